```python
import math
import jax
import jax.numpy as jnp
from jax import lax
import numpy as np

D_MODEL = 1024
BATCH = 4
SEQ = 4096
DEPTH = 4
DEC_BATCH = 32
DEC_SEQ = 8
PAST_LEN = 8192
PAGE_SIZE = 128

N_MIXERS = 3
N_DIFF_LAYERS = (DEPTH + 2) // 3
N_LRU_LAYERS = (DEPTH + 1) // 3
N_DSA_LAYERS = DEPTH // 3

ROPE_THETA = 10000.0
NORM_EPS = 1e-6
Q_BLOCK = 128

DIFF_HEAD_DIM = 64
DIFF_HEADS = D_MODEL // (2 * DIFF_HEAD_DIM)
SUBLN_EPS = 1e-5

D_RNN = D_MODEL
LRU_BLOCKS = 4
LRU_BLOCK_W = D_RNN // LRU_BLOCKS
CONV_W = 4
LRU_C = 8.0

DSA_HEAD_DIM = 64
DSA_HEADS = D_MODEL // DSA_HEAD_DIM
DSA_KV_HEADS = 4
IDX_HEADS = 8
IDX_DIM = 64
DSA_TOPK_MAX = 256
DSA_SPLITS = [DSA_HEADS * DSA_HEAD_DIM, DSA_KV_HEADS * DSA_HEAD_DIM, DSA_KV_HEADS * DSA_HEAD_DIM,
              IDX_HEADS * IDX_DIM, IDX_DIM]
DSA_IN = sum(DSA_SPLITS) + IDX_HEADS

MOE_GROUPS = 4
EXPERTS_PER_GROUP = 4
N_EXPERTS = MOE_GROUPS * EXPERTS_PER_GROUP
MOE_TOPK = 2
EXPERT_FF = 512
MOE_BLOCK = 128

kernel_name = 'hybrid_diffattn_rglru_dsa_hmoe_step'


def rms_norm(x, g, eps=NORM_EPS):
    xf = x.astype(jnp.float32)
    y = xf * lax.rsqrt(jnp.mean(xf * xf, axis=-1, keepdims=True) + eps)
    return (y * g.astype(jnp.float32)).astype(x.dtype)


def rope(x, pos):
    dh = x.shape[-1]
    inv = ROPE_THETA ** (-jnp.arange(0, dh, 2, dtype=jnp.float32) / dh)
    ang = pos.astype(jnp.float32)[:, None] * inv[None, :]
    cos = jnp.cos(ang)[None, :, None, :]
    sin = jnp.sin(ang)[None, :, None, :]
    xf = x.astype(jnp.float32)
    x1, x2 = xf[..., :dh // 2], xf[..., dh // 2:]
    return jnp.concatenate([x1 * cos - x2 * sin, x2 * cos + x1 * sin], axis=-1).astype(x.dtype)


def gather_pages(pool, j, page_table):
    g = pool[j, page_table]
    return g.reshape(g.shape[0], g.shape[1] * g.shape[2], *g.shape[3:])


def take_rows(a, idx):
    return jax.vmap(lambda ab, ib: ab[ib])(a, idx)


def over_query_blocks(fn, pos, *qs):
    nb = pos.shape[0] // Q_BLOCK
    split = lambda a: jnp.swapaxes(a.reshape(a.shape[0], nb, Q_BLOCK, *a.shape[2:]), 0, 1)
    out = lax.map(lambda args: fn(*args), (pos.reshape(nb, Q_BLOCK),) + tuple(split(a) for a in qs))
    out = jnp.swapaxes(out, 0, 1)
    return out.reshape(out.shape[0], nb * Q_BLOCK, *out.shape[3:])


def diff_project(h, pos, w_in):
    B, T, _ = h.shape
    q, k, v = jnp.split(h @ w_in, 3, axis=-1)
    q = rope(q.reshape(B, T, 2 * DIFF_HEADS, DIFF_HEAD_DIM), pos)
    k = rope(k.reshape(B, T, 2 * DIFF_HEADS, DIFF_HEAD_DIM), pos)
    v = v.reshape(B, T, DIFF_HEADS, 2 * DIFF_HEAD_DIM)
    return q, k, v


def diff_lambda_value(lam_p, lam_init):
    lp = lam_p.astype(jnp.float32)
    return jnp.exp(jnp.sum(lp[0] * lp[1])) - jnp.exp(jnp.sum(lp[2] * lp[3])) + lam_init


def diff_attend(q, segs, q_pos, lam):
    scale = DIFF_HEAD_DIM ** -0.5
    s = jnp.concatenate(
        [jnp.where(kp[None, None, None, :] <= q_pos[None, None, :, None],
                   jnp.einsum('bqhd,bkhd->bhqk', q, k, preferred_element_type=jnp.float32) * scale,
                   -jnp.inf)
         for k, _, kp in segs], axis=-1)
    p = jax.nn.softmax(s, axis=-1)
    b, _, nq, nk = p.shape
    p = p.reshape(b, DIFF_HEADS, 2, nq, nk)
    a = p[:, :, 0] - lam * p[:, :, 1]
    out, off = None, 0
    for _, v, kp in segs:
        n = kp.shape[0]
        o = jnp.einsum('bhqk,bkhe->bqhe', a[..., off:off + n].astype(v.dtype), v)
        out = o if out is None else out + o
        off += n
    return out


def diff_finish(o, subln, lam_init, w_out):
    o = rms_norm(o, subln, SUBLN_EPS) * (1.0 - lam_init)
    B, T = o.shape[:2]
    return o.reshape(B, T, D_MODEL) @ w_out


def diff_prompt(h, pos, lam, lam_init, w_in, subln, w_out):
    q, k, v = diff_project(h, pos, w_in)
    o = over_query_blocks(lambda qp, qb: diff_attend(qb, [(k, v, pos)], qp, lam), pos, q)
    return diff_finish(o, subln, lam_init, w_out), k, v


def diff_sample(h, pos, pool_k, pool_v, j, page_table, lam, lam_init, w_in, subln, w_out):
    q, k, v = diff_project(h, pos, w_in)
    past_pos = jnp.arange(page_table.shape[1] * PAGE_SIZE, dtype=jnp.int32)
    segs = [(gather_pages(pool_k, j, page_table), gather_pages(pool_v, j, page_table), past_pos),
            (k, v, pos)]
    o = diff_attend(q, segs, pos, lam)
    return diff_finish(o, subln, lam_init, w_out), k, v


def lru_mixer(h, conv_prev, h_prev, w_in, conv_w, conv_b, w_r, b_r, w_i, b_i, a_param, w_out):
    B, T, _ = h.shape
    gate, xr = jnp.split(h @ w_in, 2, axis=-1)
    xp = jnp.concatenate([conv_prev.astype(xr.dtype), xr], axis=1)
    u = conv_b + sum(xp[:, j:j + T] * conv_w[j] for j in range(CONV_W))
    new_conv = xp[:, T:]
    ub = u.reshape(B, T, LRU_BLOCKS, LRU_BLOCK_W)
    r = jax.nn.sigmoid((jnp.einsum('btnk,nkj->btnj', ub, w_r).reshape(B, T, D_RNN) + b_r).astype(jnp.float32))
    i = jax.nn.sigmoid((jnp.einsum('btnk,nkj->btnj', ub, w_i).reshape(B, T, D_RNN) + b_i).astype(jnp.float32))
    log_a = LRU_C * r * jax.nn.log_sigmoid(a_param.astype(jnp.float32))
    a = jnp.exp(log_a)
    bx = jnp.sqrt(-jnp.expm1(2.0 * log_a)) * i * u.astype(jnp.float32)

    def step(hc, ab):
        hc = ab[0] * hc + ab[1]
        return hc, hc

    h_last, hs = lax.scan(step, h_prev.astype(jnp.float32), (jnp.swapaxes(a, 0, 1), jnp.swapaxes(bx, 0, 1)))
    hs = jnp.swapaxes(hs, 0, 1)
    y = (jax.nn.gelu(gate.astype(jnp.float32), approximate=True) * hs).astype(h.dtype) @ w_out
    return y, new_conv, h_last.astype(h.dtype)


def dsa_project(h, pos, w_in):
    B, T, _ = h.shape
    q, k, v, qi, ki, wi = jnp.split(h @ w_in, np.cumsum(DSA_SPLITS).tolist(), axis=-1)
    q = rope(q.reshape(B, T, DSA_HEADS, DSA_HEAD_DIM), pos)
    k = rope(k.reshape(B, T, DSA_KV_HEADS, DSA_HEAD_DIM), pos)
    v = v.reshape(B, T, DSA_KV_HEADS, DSA_HEAD_DIM)
    qi = rope(qi.reshape(B, T, IDX_HEADS, IDX_DIM), pos)
    ki = rope(ki.reshape(B, T, 1, IDX_DIM), pos)[:, :, 0]
    return q, k, v, qi, ki, wi


def dsa_select(qi, wi, ki, q_pos, k_pos, topk):
    dots = jnp.einsum('bqhd,bsd->bqhs', qi, ki, preferred_element_type=jnp.float32) * IDX_DIM ** -0.5
    score = jnp.einsum('bqhs,bqh->bqs', jax.nn.relu(dots), wi.astype(jnp.float32)) * IDX_HEADS ** -0.5
    adm = k_pos[None, None, :] <= q_pos[None, :, None]
    _, sel = lax.top_k(jnp.where(adm, score, -jnp.inf), topk)
    valid = k_pos[sel] <= q_pos[None, :, None]
    return sel, valid


def dsa_attend(q, kg, vg, valid):
    B, Q = q.shape[:2]
    qg = q.reshape(B, Q, DSA_KV_HEADS, DSA_HEADS // DSA_KV_HEADS, DSA_HEAD_DIM)
    s = jnp.einsum('bqgrd,bqkgd->bqgrk', qg, kg, preferred_element_type=jnp.float32) * DSA_HEAD_DIM ** -0.5
    p = jax.nn.softmax(jnp.where(valid[:, :, None, None, :], s, -jnp.inf), axis=-1)
    o = jnp.einsum('bqgrk,bqkgd->bqgrd', p.astype(vg.dtype), vg)
    return o.reshape(B, Q, DSA_HEADS * DSA_HEAD_DIM)


def dsa_prompt(h, pos, w_in, w_out):
    q, k, v, qi, ki, wi = dsa_project(h, pos, w_in)
    topk = min(DSA_TOPK_MAX, h.shape[1] // 4)

    def block(qp, qb, qib, wib):
        sel, valid = dsa_select(qib, wib, ki, qp, pos, topk)
        return dsa_attend(qb, take_rows(k, sel), take_rows(v, sel), valid)

    o = over_query_blocks(block, pos, q, qi, wi)
    return o @ w_out, k, v, ki


def dsa_sample(h, pos, pool_k, pool_v, pool_kidx, j, page_table, w_in, w_out):
    q, k, v, qi, ki, wi = dsa_project(h, pos, w_in)
    n_past = page_table.shape[1] * PAGE_SIZE
    n_new = h.shape[1]
    ki_all = jnp.concatenate([gather_pages(pool_kidx, j, page_table), ki], axis=1)
    k_pos = jnp.arange(n_past + n_new, dtype=jnp.int32)
    topk = min(DSA_TOPK_MAX, (n_past + n_new) // 4)
    sel, valid = dsa_select(qi, wi, ki_all, pos, k_pos, topk)
    sp = jnp.minimum(sel, n_past - 1)
    phys = page_table[jnp.arange(page_table.shape[0])[:, None, None], sp // PAGE_SIZE]
    slot = sp % PAGE_SIZE
    sn = jnp.clip(sel - n_past, 0, n_new - 1)
    in_past = (sel < n_past)[..., None, None]
    kg = jnp.where(in_past, pool_k[j, phys, slot], take_rows(k, sn))
    vg = jnp.where(in_past, pool_v[j, phys, slot], take_rows(v, sn))
    o = dsa_attend(q, kg, vg, valid)
    return o @ w_out, k, v, ki


def moe_ffn(x, w_group, b_group, w_expert, b_expert, w_up, w_down):
    shp = x.shape
    xt = x.reshape(-1, D_MODEL)
    n = xt.shape[0]
    p_grp = jax.nn.softmax((xt @ w_group).astype(jnp.float32) + b_group.astype(jnp.float32), axis=-1)
    grp = jnp.argmax(p_grp, axis=-1)
    g_gate = jnp.take_along_axis(p_grp, grp[:, None], axis=-1)
    le = ((xt @ w_expert).astype(jnp.float32) + b_expert.astype(jnp.float32)).reshape(n, MOE_GROUPS, EXPERTS_PER_GROUP)
    le = jnp.take_along_axis(le, grp[:, None, None], axis=1)[:, 0]
    top_l, top_e = lax.top_k(le, MOE_TOPK)
    gate = g_gate * jax.nn.softmax(top_l, axis=-1)
    eid = (grp[:, None] * EXPERTS_PER_GROUP + top_e).reshape(-1)
    tok = jnp.repeat(jnp.arange(n, dtype=jnp.int32), MOE_TOPK)
    n_assign = n * MOE_TOPK
    order = jnp.argsort(eid)
    e_s, tok_s, gate_s = eid[order], tok[order], gate.reshape(-1)[order]
    counts = jnp.bincount(eid, length=N_EXPERTS)
    padded = (counts + MOE_BLOCK - 1) // MOE_BLOCK * MOE_BLOCK
    start = jnp.cumsum(counts) - counts
    pend = jnp.cumsum(padded)
    pstart = pend - padded
    dest = pstart[e_s] + jnp.arange(n_assign) - start[e_s]
    n_blocks = (n_assign + N_EXPERTS * (MOE_BLOCK - 1) + MOE_BLOCK - 1) // MOE_BLOCK
    block_e = jnp.minimum(jnp.searchsorted(pend, jnp.arange(n_blocks) * MOE_BLOCK, side='right'), N_EXPERTS - 1)
    xbuf = jnp.zeros((n_blocks * MOE_BLOCK, D_MODEL), x.dtype).at[dest].set(xt[tok_s])

    def expert(args):
        xb, e = args
        g, u = jnp.split(xb @ w_up[e], 2, axis=-1)
        return (jax.nn.silu(g) * u) @ w_down[e]

    ybuf = lax.map(expert, (xbuf.reshape(n_blocks, MOE_BLOCK, D_MODEL), block_e)).reshape(-1, D_MODEL)
    y = jax.ops.segment_sum(ybuf[dest] * gate_s[:, None].astype(x.dtype), tok_s, num_segments=n)
    return y.reshape(shp)


def setup_inputs(seed: int = 0) -> dict:
    key = jax.random.key(seed)
    ks = iter(jax.random.split(key, 48))

    def nrm(shape, scale=1.0):
        return jax.random.normal(next(ks), shape, jnp.float32) * scale

    n_pages = PAST_LEN // PAGE_SIZE
    n_used = DEC_BATCH * n_pages
    n_pool = n_used + n_used // 4
    d = D_MODEL ** -0.5
    lo = math.log(0.9 ** 0.125 / (1.0 - 0.9 ** 0.125))
    hi = math.log(0.999 ** 0.125 / (1.0 - 0.999 ** 0.125))
    inp = {}
    inp['x_prompt'] = nrm((BATCH, SEQ, D_MODEL))
    inp['x_sample'] = nrm((DEC_BATCH, DEC_SEQ, D_MODEL))
    inp['cache_diff_k'] = nrm((N_DIFF_LAYERS, n_pool, PAGE_SIZE, 2 * DIFF_HEADS, DIFF_HEAD_DIM))
    inp['cache_diff_v'] = nrm((N_DIFF_LAYERS, n_pool, PAGE_SIZE, DIFF_HEADS, 2 * DIFF_HEAD_DIM))
    inp['state_lru_h'] = nrm((N_LRU_LAYERS, DEC_BATCH, D_RNN), 0.5)
    inp['state_lru_conv'] = nrm((N_LRU_LAYERS, DEC_BATCH, CONV_W - 1, D_RNN))
    inp['cache_dsa_k'] = nrm((N_DSA_LAYERS, n_pool, PAGE_SIZE, DSA_KV_HEADS, DSA_HEAD_DIM))
    inp['cache_dsa_v'] = nrm((N_DSA_LAYERS, n_pool, PAGE_SIZE, DSA_KV_HEADS, DSA_HEAD_DIM))
    inp['cache_dsa_kidx'] = nrm((N_DSA_LAYERS, n_pool, PAGE_SIZE, IDX_DIM))
    inp['page_table'] = jax.random.permutation(next(ks), n_pool)[:n_used].reshape(DEC_BATCH, n_pages).astype(jnp.int32)
    inp['norm_mix'] = 1.0 + nrm((DEPTH, D_MODEL), 0.02)
    inp['norm_ffn'] = 1.0 + nrm((DEPTH, D_MODEL), 0.02)
    inp['norm_final'] = 1.0 + nrm((D_MODEL,), 0.02)
    inp['diff_w_in'] = nrm((N_DIFF_LAYERS, D_MODEL, 3 * D_MODEL), d)
    inp['diff_lambda'] = nrm((N_DIFF_LAYERS, 4, DIFF_HEAD_DIM), 0.1)
    inp['diff_subln'] = 1.0 + nrm((N_DIFF_LAYERS, 2 * DIFF_HEAD_DIM), 0.02)
    inp['diff_w_out'] = nrm((N_DIFF_LAYERS, D_MODEL, D_MODEL), d)
    inp['lru_w_in'] = nrm((N_LRU_LAYERS, D_MODEL, 2 * D_RNN), d)
    inp['lru_conv_w'] = nrm((N_LRU_LAYERS, CONV_W, D_RNN), CONV_W ** -0.5)
    inp['lru_conv_b'] = nrm((N_LRU_LAYERS, D_RNN), 0.01)
    inp['lru_w_r'] = nrm((N_LRU_LAYERS, LRU_BLOCKS, LRU_BLOCK_W, LRU_BLOCK_W), LRU_BLOCK_W ** -0.5)
    inp['lru_b_r'] = nrm((N_LRU_LAYERS, D_RNN), 0.01)
    inp['lru_w_i'] = nrm((N_LRU_LAYERS, LRU_BLOCKS, LRU_BLOCK_W, LRU_BLOCK_W), LRU_BLOCK_W ** -0.5)
    inp['lru_b_i'] = nrm((N_LRU_LAYERS, D_RNN), 0.01)
    inp['lru_a'] = jax.random.uniform(next(ks), (N_LRU_LAYERS, D_RNN), jnp.float32, lo, hi)
    inp['lru_w_out'] = nrm((N_LRU_LAYERS, D_RNN, D_MODEL), D_RNN ** -0.5)
    inp['dsa_w_in'] = nrm((N_DSA_LAYERS, D_MODEL, DSA_IN), d)
    inp['dsa_w_out'] = nrm((N_DSA_LAYERS, D_MODEL, D_MODEL), d)
    inp['moe_w_group'] = nrm((DEPTH, D_MODEL, MOE_GROUPS), d)
    inp['moe_b_group'] = nrm((DEPTH, MOE_GROUPS), 0.01)
    inp['moe_w_expert'] = nrm((DEPTH, D_MODEL, N_EXPERTS), d)
    inp['moe_b_expert'] = nrm((DEPTH, N_EXPERTS), 0.01)
    inp['moe_w_up'] = nrm((DEPTH, N_EXPERTS, D_MODEL, 2 * EXPERT_FF), d)
    inp['moe_w_down'] = nrm((DEPTH, N_EXPERTS, EXPERT_FF, D_MODEL), EXPERT_FF ** -0.5)
    return inp


def reference(x_prompt, x_sample, cache_diff_k, cache_diff_v, state_lru_h, state_lru_conv,
              cache_dsa_k, cache_dsa_v, cache_dsa_kidx, page_table,
              norm_mix, norm_ffn, norm_final,
              diff_w_in, diff_lambda, diff_subln, diff_w_out,
              lru_w_in, lru_conv_w, lru_conv_b, lru_w_r, lru_b_r, lru_w_i, lru_b_i, lru_a, lru_w_out,
              dsa_w_in, dsa_w_out,
              moe_w_group, moe_b_group, moe_w_expert, moe_b_expert, moe_w_up, moe_w_down):
    pos_p = jnp.arange(x_prompt.shape[1], dtype=jnp.int32)
    pos_s = PAST_LEN + jnp.arange(x_sample.shape[1], dtype=jnp.int32)
    xp, xs = x_prompt, x_sample
    dk_p, dv_p, dk_s, dv_s = [], [], [], []
    lh_p, lc_p, lh_s, lc_s = [], [], [], []
    sk_p, sv_p, si_p, sk_s, sv_s, si_s = [], [], [], [], [], []
    for i in range(DEPTH):
        kind, j = i % N_MIXERS, i // N_MIXERS
        hp = rms_norm(xp, norm_mix[i])
        hs = rms_norm(xs, norm_mix[i])
        if kind == 0:
            lam_init = 0.8 - 0.6 * math.exp(-0.3 * i)
            lam = diff_lambda_value(diff_lambda[j], lam_init)
            mp, k, v = diff_prompt(hp, pos_p, lam, lam_init, diff_w_in[j], diff_subln[j], diff_w_out[j])
            dk_p.append(k)
            dv_p.append(v)
            ms, k, v = diff_sample(hs, pos_s, cache_diff_k, cache_diff_v, j, page_table, lam, lam_init,
                                   diff_w_in[j], diff_subln[j], diff_w_out[j])
            dk_s.append(k)
            dv_s.append(v)
        elif kind == 1:
            lru_w = (lru_w_in[j], lru_conv_w[j], lru_conv_b[j], lru_w_r[j], lru_b_r[j],
                     lru_w_i[j], lru_b_i[j], lru_a[j], lru_w_out[j])
            zc = jnp.zeros((hp.shape[0], CONV_W - 1, D_RNN), hp.dtype)
            zh = jnp.zeros((hp.shape[0], D_RNN), hp.dtype)
            mp, c, h = lru_mixer(hp, zc, zh, *lru_w)
            lc_p.append(c)
            lh_p.append(h)
            ms, c, h = lru_mixer(hs, state_lru_conv[j], state_lru_h[j], *lru_w)
            lc_s.append(c)
            lh_s.append(h)
        else:
            mp, k, v, ki = dsa_prompt(hp, pos_p, dsa_w_in[j], dsa_w_out[j])
            sk_p.append(k)
            sv_p.append(v)
            si_p.append(ki)
            ms, k, v, ki = dsa_sample(hs, pos_s, cache_dsa_k, cache_dsa_v, cache_dsa_kidx, j, page_table,
                                      dsa_w_in[j], dsa_w_out[j])
            sk_s.append(k)
            sv_s.append(v)
            si_s.append(ki)
        xp = xp + mp
        xs = xs + ms
        moe_w = (moe_w_group[i], moe_b_group[i], moe_w_expert[i], moe_b_expert[i], moe_w_up[i], moe_w_down[i])
        xp = xp + moe_ffn(rms_norm(xp, norm_ffn[i]), *moe_w)
        xs = xs + moe_ffn(rms_norm(xs, norm_ffn[i]), *moe_w)
    y_prompt = rms_norm(xp, norm_final)
    y_sample = rms_norm(xs, norm_final)
    return (y_prompt, y_sample,
            jnp.stack(dk_p), jnp.stack(dv_p), jnp.stack(dk_s), jnp.stack(dv_s),
            jnp.stack(lh_p), jnp.stack(lc_p), jnp.stack(lh_s), jnp.stack(lc_s),
            jnp.stack(sk_p), jnp.stack(sv_p), jnp.stack(si_p),
            jnp.stack(sk_s), jnp.stack(sv_s), jnp.stack(si_s))
```

```python
import functools
import math

import jax
import jax.numpy as jnp
from jax import lax
from jax.experimental import pallas as pl
from jax.experimental.pallas import tpu as pltpu

F32 = jnp.float32
BF16 = jnp.bfloat16
I32 = jnp.int32

LANE_V7X = 128
SUBLANE_V7X = 8
VMEM_LIMIT_BYTES = 48 * 1024 * 1024

D_MODEL = 1024
PAST_LEN = 8192
PAGE = 128
ROPE_THETA = 10000.0
NORM_EPS = 1e-6
SUBLN_EPS = 1e-5
HEAD_DIM = 64
DIFF_HEADS = 8
LRU_BLOCKS = 4
LRU_BLOCK_W = D_MODEL // LRU_BLOCKS
CONV_W = 4
DSA_HEADS = 16
DSA_KV_HEADS = 4
IDX_HEADS = 8
DSA_TOPK_MAX = 256
N_EXPERTS = 16
EXPERT_FF = 512
MOE_ROWS = 256

NEG = -1e30
INT_MIN = -(2 ** 31)

_NT = (((1,), (1,)), ((), ()))


def _cparams(sem):
    return pltpu.CompilerParams(dimension_semantics=sem, vmem_limit_bytes=VMEM_LIMIT_BYTES)


def _norm_matmul_kernel(n_rope_steps, x_ref, g_ref, w_ref, cos_ref, sin_ref, of_ref, ob_ref, xn_ref):
    j = pl.program_id(1)

    @pl.when(j == 0)
    def _():
        x = x_ref[...]
        inv = lax.rsqrt(jnp.mean(x * x, axis=-1, keepdims=True) + NORM_EPS)
        xn_ref[...] = (x * inv * g_ref[...]).astype(BF16)

    acc = jnp.dot(xn_ref[...], w_ref[...], preferred_element_type=F32)
    tn = acc.shape[1]

    def store(val):
        of_ref[...] = val
        ob_ref[...] = val.astype(BF16)

    if n_rope_steps == 0:
        store(acc)
        return

    @pl.when(j < n_rope_steps)
    def _():
        cos = cos_ref[...]
        sin = sin_ref[...]
        lane = lax.broadcasted_iota(I32, cos.shape, 1)
        first = (lane % HEAD_DIM) < (HEAD_DIM // 2)
        parts = []
        for c in range(tn // LANE_V7X):
            a = acc[:, c * LANE_V7X:(c + 1) * LANE_V7X]
            rot = jnp.where(first, pltpu.roll(a, LANE_V7X - HEAD_DIM // 2, 1), pltpu.roll(a, HEAD_DIM // 2, 1))
            parts.append(a * cos + rot * sin)
        store(jnp.concatenate(parts, axis=1))

    @pl.when(j >= n_rope_steps)
    def _():
        store(acc)


def norm_matmul(x, g, w, cos, sin, *, n_rope_cols, tm, tn):
    n, d = x.shape
    m = w.shape[1]
    assert n % tm == 0 and m % tn == 0 and n_rope_cols % tn == 0
    kern = functools.partial(_norm_matmul_kernel, n_rope_cols // tn)
    return pl.pallas_call(
        kern,
        grid=(n // tm, m // tn),
        in_specs=[
            pl.BlockSpec((tm, d), lambda i, j: (i, 0)),
            pl.BlockSpec((1, d), lambda i, j: (0, 0)),
            pl.BlockSpec((d, tn), lambda i, j: (0, j)),
            pl.BlockSpec((tm, LANE_V7X), lambda i, j: (i, 0)),
            pl.BlockSpec((tm, LANE_V7X), lambda i, j: (i, 0)),
        ],
        out_specs=[
            pl.BlockSpec((tm, tn), lambda i, j: (i, j)),
            pl.BlockSpec((tm, tn), lambda i, j: (i, j)),
        ],
        out_shape=[jax.ShapeDtypeStruct((n, m), F32), jax.ShapeDtypeStruct((n, m), BF16)],
        scratch_shapes=[pltpu.VMEM((tm, d), BF16)],
        compiler_params=_cparams(("parallel", "arbitrary")),
        name="norm_matmul",
    )(x, g.reshape(1, d), w, cos, sin)


def _matmul_res_kernel(a_ref, w_ref, r_ref, o_ref):
    o_ref[...] = r_ref[...] + jnp.dot(a_ref[...], w_ref[...], preferred_element_type=F32)


def matmul_residual(a, w, res, *, tm):
    n, k = a.shape
    m = w.shape[1]
    assert n % tm == 0
    return pl.pallas_call(
        _matmul_res_kernel,
        grid=(n // tm,),
        in_specs=[
            pl.BlockSpec((tm, k), lambda i: (i, 0)),
            pl.BlockSpec((k, m), lambda i: (0, 0)),
            pl.BlockSpec((tm, m), lambda i: (i, 0)),
        ],
        out_specs=pl.BlockSpec((tm, m), lambda i: (i, 0)),
        out_shape=jax.ShapeDtypeStruct((n, m), F32),
        compiler_params=_cparams(("parallel",)),
        name="matmul_residual",
    )(a, w, res)


def _online_softmax_update(s, sel, v, m_ref, l_ref, acc_ref, idx=None):
    m_prev = m_ref[...] if idx is None else m_ref[idx]
    m_new = jnp.maximum(m_prev, jnp.max(s, axis=1, keepdims=True))
    alpha = jnp.exp(m_prev - m_new)
    p = jnp.exp(s - m_new)
    if sel is not None:
        p = jnp.where(sel, p, 0.0)
    pv = jnp.dot(p.astype(BF16), v, preferred_element_type=F32)
    if idx is None:
        l_ref[...] = alpha * l_ref[...] + jnp.sum(p, axis=1, keepdims=True)
        acc_ref[...] = alpha * acc_ref[...] + pv
        m_ref[...] = m_new
    else:
        l_ref[idx] = alpha * l_ref[idx] + jnp.sum(p, axis=1, keepdims=True)
        acc_ref[idx] = alpha * acc_ref[idx] + pv
        m_ref[idx] = m_new


def _diff_prompt_kernel(tq, tk, out_scale, lam_ref, q_ref, k_ref, v_ref, sub_ref, o_ref, acc_ref, m_ref, l_ref):
    i = pl.program_id(2)
    q = q_ref[...]
    lane = lax.broadcasted_iota(I32, q.shape, 1)
    zero = jnp.zeros_like(q)
    qq = jnp.concatenate([jnp.where(lane < HEAD_DIM, q, zero), jnp.where(lane >= HEAD_DIM, q, zero)], axis=0)
    m_ref[...] = jnp.full(m_ref.shape, NEG, F32)
    l_ref[...] = jnp.zeros(l_ref.shape, F32)
    acc_ref[...] = jnp.zeros(acc_ref.shape, F32)

    def step(kb, masked):
        start = pl.multiple_of(kb * tk, tk)
        k = k_ref[pl.ds(start, tk), :]
        v = v_ref[pl.ds(start, tk), :]
        s = lax.dot_general(qq, k, _NT, preferred_element_type=F32)
        sel = None
        if masked:
            row = lax.broadcasted_iota(I32, s.shape, 0)
            col = lax.broadcasted_iota(I32, s.shape, 1)
            sel = (kb * tk + col) <= (i * tq + (row % tq))
            s = jnp.where(sel, s, NEG)
        _online_softmax_update(s, sel, v, m_ref, l_ref, acc_ref)

    n_full = (i * tq) // tk

    def body(kb, carry):
        step(kb, False)
        return carry

    lax.fori_loop(0, n_full, body, 0)
    for d in range(max(tq // tk, 1)):
        step(n_full + d, True)

    acc = acc_ref[...]
    l = l_ref[...]
    o = acc[:tq] / l[:tq] - lam_ref[...] * (acc[tq:] / l[tq:])
    o = o * lax.rsqrt(jnp.mean(o * o, axis=-1, keepdims=True) + SUBLN_EPS) * sub_ref[...] * out_scale
    o_ref[...] = o.astype(BF16)


def diff_attention_prompt(qkv, lam, subln, out_scale, *, batch, seq, tq, tk):
    nq = seq // tq
    kern = functools.partial(_diff_prompt_kernel, tq, tk, out_scale)
    return pl.pallas_call(
        kern,
        grid=(batch, DIFF_HEADS, nq),
        in_specs=[
            pl.BlockSpec((1, LANE_V7X), lambda b, h, i: (0, 0)),
            pl.BlockSpec((tq, LANE_V7X), lambda b, h, i: (b * nq + i, h)),
            pl.BlockSpec((seq, LANE_V7X), lambda b, h, i: (b, DIFF_HEADS + h)),
            pl.BlockSpec((seq, LANE_V7X), lambda b, h, i: (b, 2 * DIFF_HEADS + h)),
            pl.BlockSpec((1, LANE_V7X), lambda b, h, i: (0, 0)),
        ],
        out_specs=pl.BlockSpec((tq, LANE_V7X), lambda b, h, i: (b * nq + i, h)),
        out_shape=jax.ShapeDtypeStruct((batch * seq, D_MODEL), BF16),
        scratch_shapes=[
            pltpu.VMEM((2 * tq, LANE_V7X), F32),
            pltpu.VMEM((2 * tq, 1), F32),
            pltpu.VMEM((2 * tq, 1), F32),
        ],
        compiler_params=_cparams(("parallel", "parallel", "arbitrary")),
        name="diff_attention_prompt",
    )(jnp.full((1, LANE_V7X), lam, F32), qkv, qkv, qkv, subln.reshape(1, LANE_V7X))


ROWS_S = 128


def _paged_attn_kernel(mode, pages_per_step, n_page_steps, out_scale, pt_ref, *refs):
    P = pages_per_step
    q_ref = refs[0]
    kp = refs[1:1 + P]
    vp = refs[1 + P:1 + 2 * P]
    knew_ref, vnew_ref = refs[1 + 2 * P], refs[2 + 2 * P]
    rest = refs[3 + 2 * P:]
    if mode == "diff":
        lam_ref, sub_ref, o_ref, acc_ref, m_ref, l_ref = rest
        mask_ref = None
    else:
        mask_ref, o_ref, acc_ref, m_ref, l_ref = rest
    s_id = pl.program_id(1)

    @pl.when(s_id == 0)
    def _():
        m_ref[...] = jnp.full(m_ref.shape, NEG, F32)
        l_ref[...] = jnp.zeros(l_ref.shape, F32)
        acc_ref[...] = jnp.zeros(acc_ref.shape, F32)

    q = q_ref[0]

    def attend(k, v, sel):
        s = lax.dot_general(q, k.astype(BF16), _NT, preferred_element_type=F32)
        if sel is not None:
            s = jnp.where(sel, s, NEG)
        _online_softmax_update(s, sel, v.astype(BF16), m_ref, l_ref, acc_ref)

    def page_sel(page):
        if mask_ref is None:
            return None
        return jnp.tile(mask_ref[0, page], (ROWS_S // SUBLANE_V7X, 1)) > 0.5

    @pl.when(s_id < n_page_steps)
    def _():
        for p in range(P):
            attend(kp[p][0], vp[p][0], page_sel(s_id * P + p))

    @pl.when(s_id == n_page_steps)
    def _():
        if mask_ref is None:
            row = lax.broadcasted_iota(I32, (ROWS_S, PAGE), 0)
            col = lax.broadcasted_iota(I32, (ROWS_S, PAGE), 1)
            sel = col <= (row % SUBLANE_V7X)
        else:
            sel = page_sel(n_page_steps * P)
        attend(knew_ref[0], vnew_ref[0], sel)
        o = acc_ref[...] / l_ref[...]
        if mode == "diff":
            for h in range(DIFF_HEADS):
                cols = slice(h * LANE_V7X, (h + 1) * LANE_V7X)
                o1 = o[(2 * h) * SUBLANE_V7X:(2 * h + 1) * SUBLANE_V7X, cols]
                o2 = o[(2 * h + 1) * SUBLANE_V7X:(2 * h + 2) * SUBLANE_V7X, cols]
                oh = o1 - lam_ref[...] * o2
                oh = oh * lax.rsqrt(jnp.mean(oh * oh, axis=-1, keepdims=True) + SUBLN_EPS) * sub_ref[...] * out_scale
                o_ref[0, :, cols] = oh
        else:
            o_ref[0] = o


def paged_attention(mode, q_bd, pool_k, pool_v, layer, page_table, k_new, v_new, *, pages_per_step,
                    mask=None, lam=None, subln=None, out_scale=1.0):
    n_seq, n_pages = page_table.shape
    dk = q_bd.shape[2]
    dv = pool_v.shape[2]
    n_pool = pool_k.shape[0]
    P = pages_per_step
    assert n_pages % P == 0
    nps = n_pages // P
    pt = (page_table + layer).reshape(-1).astype(I32)
    del n_pool

    def page_map(p):
        def f(b, s, pt_ref):
            return (pt_ref[b * n_pages + jnp.minimum(s, nps - 1) * P + p], 0, 0)
        return f

    in_specs = [pl.BlockSpec((1, ROWS_S, dk), lambda b, s, pt_ref: (b, 0, 0))]
    in_specs += [pl.BlockSpec((1, PAGE, dk), page_map(p)) for p in range(P)]
    in_specs += [pl.BlockSpec((1, PAGE, dv), page_map(p)) for p in range(P)]
    in_specs += [pl.BlockSpec((1, PAGE, dk), lambda b, s, pt_ref: (b, 0, 0)),
                 pl.BlockSpec((1, PAGE, dv), lambda b, s, pt_ref: (b, 0, 0))]
    args = [q_bd] + [pool_k] * P + [pool_v] * P + [k_new, v_new]
    if mode == "diff":
        in_specs += [pl.BlockSpec((1, LANE_V7X), lambda b, s, pt_ref: (0, 0)),
                     pl.BlockSpec((1, LANE_V7X), lambda b, s, pt_ref: (0, 0))]
        args += [jnp.full((1, LANE_V7X), lam, F32), subln.reshape(1, LANE_V7X)]
        out_shape = jax.ShapeDtypeStruct((n_seq, SUBLANE_V7X, dv), F32)
        out_spec = pl.BlockSpec((1, SUBLANE_V7X, dv), lambda b, s, pt_ref: (b, 0, 0))
    else:
        in_specs += [pl.BlockSpec((1, n_pages + 1, SUBLANE_V7X, PAGE), lambda b, s, pt_ref: (b, 0, 0, 0))]
        args += [mask]
        out_shape = jax.ShapeDtypeStruct((n_seq, ROWS_S, dv), F32)
        out_spec = pl.BlockSpec((1, ROWS_S, dv), lambda b, s, pt_ref: (b, 0, 0))
    kern = functools.partial(_paged_attn_kernel, mode, P, nps, out_scale)
    return pl.pallas_call(
        kern,
        grid_spec=pltpu.PrefetchScalarGridSpec(
            num_scalar_prefetch=1,
            grid=(n_seq, nps + 1),
            in_specs=in_specs,
            out_specs=out_spec,
            scratch_shapes=[pltpu.VMEM((ROWS_S, dv), F32), pltpu.VMEM((ROWS_S, 1), F32), pltpu.VMEM((ROWS_S, 1), F32)],
        ),
        out_shape=out_shape,
        compiler_params=_cparams(("parallel", "arbitrary")),
        name="paged_attention_" + mode,
    )(pt, *args)


def _shift_rows(x, prev8, j, row):
    r = pltpu.roll(x, j, 1)
    p = pltpu.roll(prev8, j, 1)
    if x.shape[1] > SUBLANE_V7X:
        p = jnp.concatenate([p, r[:, SUBLANE_V7X:]], axis=1)
    return jnp.where(row < j, p, r)


def _lru_kernel(gx_ref, h0_ref, c0_ref, cw_ref, cb_ref, wr_ref, br_ref, wi_ref, bi_ref, la_ref,
                y_ref, hout_ref, cout_ref, h_ref, c_ref):
    t_id = pl.program_id(1)

    @pl.when(t_id == 0)
    def _():
        h_ref[...] = h0_ref[...]
        c_ref[...] = c0_ref[...]

    gb, tm, _ = gx_ref.shape
    d = D_MODEL
    gate = gx_ref[:, :, :d]
    xr = gx_ref[:, :, d:]
    row = lax.broadcasted_iota(I32, (gb, tm, d), 1)
    prev8 = c_ref[...]
    u = cb_ref[...] + xr * cw_ref[CONV_W - 1]
    for j in range(1, CONV_W):
        u = u + _shift_rows(xr, prev8, j, row) * cw_ref[CONV_W - 1 - j]
    c_ref[...] = xr[:, tm - SUBLANE_V7X:, :]

    u2 = u.reshape(gb * tm, d).astype(BF16)
    rs, is_ = [], []
    for n in range(LRU_BLOCKS):
        ub = u2[:, n * LRU_BLOCK_W:(n + 1) * LRU_BLOCK_W]
        rs.append(jnp.dot(ub, wr_ref[n], preferred_element_type=F32))
        is_.append(jnp.dot(ub, wi_ref[n], preferred_element_type=F32))
    r = jax.nn.sigmoid(jnp.concatenate(rs, axis=1).reshape(gb, tm, d) + br_ref[...])
    ig = jax.nn.sigmoid(jnp.concatenate(is_, axis=1).reshape(gb, tm, d) + bi_ref[...])
    log_a = r * la_ref[...]
    a = jnp.exp(log_a)
    b = jnp.sqrt(1.0 - a * a) * ig * u

    dist = 1
    while dist < tm:
        a_sh = jnp.where(row >= dist, pltpu.roll(a, dist, 1), 1.0)
        b_sh = jnp.where(row >= dist, pltpu.roll(b, dist, 1), 0.0)
        b = b + a * b_sh
        a = a * a_sh
        dist *= 2
    hs = b + a * h_ref[...]
    h_ref[...] = hs[:, tm - 1:tm, :]
    y_ref[...] = (jax.nn.gelu(gate, approximate=True) * hs).astype(BF16)

    @pl.when(t_id == pl.num_programs(1) - 1)
    def _():
        hout_ref[...] = h_ref[...]
        cout_ref[...] = c_ref[...]


def lru_core(gx, h0, c0, conv_w, conv_b, w_r, b_r, w_i, b_i, la, *, gb, tm):
    g, t, _ = gx.shape
    d = D_MODEL
    assert g % gb == 0 and t % tm == 0 and tm % SUBLANE_V7X == 0
    vec = lambda a: a.reshape(1, 1, d)
    full3 = lambda shp: pl.BlockSpec(shp, lambda i, s: (0, 0, 0))
    return pl.pallas_call(
        _lru_kernel,
        grid=(g // gb, t // tm),
        in_specs=[
            pl.BlockSpec((gb, tm, 2 * d), lambda i, s: (i, s, 0)),
            pl.BlockSpec((gb, 1, d), lambda i, s: (i, 0, 0)),
            pl.BlockSpec((gb, SUBLANE_V7X, d), lambda i, s: (i, 0, 0)),
            full3((CONV_W, 1, d)),
            full3((1, 1, d)),
            full3((LRU_BLOCKS, LRU_BLOCK_W, LRU_BLOCK_W)),
            full3((1, 1, d)),
            full3((LRU_BLOCKS, LRU_BLOCK_W, LRU_BLOCK_W)),
            full3((1, 1, d)),
            full3((1, 1, d)),
        ],
        out_specs=[
            pl.BlockSpec((gb, tm, d), lambda i, s: (i, s, 0)),
            pl.BlockSpec((gb, 1, d), lambda i, s: (i, 0, 0)),
            pl.BlockSpec((gb, SUBLANE_V7X, d), lambda i, s: (i, 0, 0)),
        ],
        out_shape=[
            jax.ShapeDtypeStruct((g, t, d), BF16),
            jax.ShapeDtypeStruct((g, 1, d), F32),
            jax.ShapeDtypeStruct((g, SUBLANE_V7X, d), F32),
        ],
        scratch_shapes=[pltpu.VMEM((gb, 1, d), F32), pltpu.VMEM((gb, SUBLANE_V7X, d), F32)],
        compiler_params=_cparams(("parallel", "arbitrary")),
        name="lru_core",
    )(gx, h0, c0, conv_w.reshape(CONV_W, 1, d), vec(conv_b), w_r.astype(BF16), vec(b_r), w_i.astype(BF16), vec(b_i), vec(la))


def _sortable_key(score):
    bits = pltpu.bitcast(score, I32)
    key = bits ^ ((bits >> 31) & 0x7FFFFFFF)
    return jnp.where(score == 0.0, 0, key)


def _kth_largest_key(count_ge, rows, topk):
    def body(it, t_cur):
        cand = t_cur ^ lax.shift_left(jnp.int32(1), 31 - it)
        return jnp.where(count_ge(cand) >= topk, cand, t_cur)
    return lax.fori_loop(0, 32, body, jnp.full((rows, 1), INT_MIN, I32))


def _tie_select(keys, t_sel, need, tie_count, upper):
    eq = keys == t_sel
    eqf = jnp.where(eq, 1.0, 0.0)
    pref = jnp.dot(eqf.astype(BF16), upper, preferred_element_type=F32)
    take = jnp.where(eq, jnp.where(tie_count + pref < need, 1.0, 0.0), 0.0)
    self = jnp.where(keys > t_sel, 1.0, take)
    self = jnp.where(keys == INT_MIN, 0.0, self)
    return self, tie_count + jnp.sum(eqf, axis=1, keepdims=True)


def _strict_upper(n):
    return jnp.where(lax.broadcasted_iota(I32, (n, n), 0) < lax.broadcasted_iota(I32, (n, n), 1), 1.0, 0.0).astype(BF16)


def _dsa_prompt_kernel(tq, tk, topk, q_ref, qi_ref, wi_ref, k_ref, v_ref, ki_ref, o_ref,
                       key_ref, qs_ref, acc_ref, m_ref, l_ref):
    i = pl.program_id(1)
    n_kb = (i * tq) // tk + 1
    lane = lax.broadcasted_iota(I32, (tq, LANE_V7X), 1)
    low = lane < HEAD_DIM

    qi = qi_ref[...]
    wi = wi_ref[...]
    qih, wih = [], []
    for h in range(IDX_HEADS):
        t = qi[:, (h // 2) * LANE_V7X:(h // 2 + 1) * LANE_V7X]
        if h % 2:
            t = pltpu.roll(t, HEAD_DIM, 1)
        qih.append(t.astype(BF16))
        wih.append(wi[:, h:h + 1])

    def a_step(kb, carry):
        start = pl.multiple_of(kb * tk, tk)
        ki = ki_ref[pl.ds(start, tk), :]
        sc = jnp.zeros((tq, tk), F32)
        for h in range(IDX_HEADS):
            dd = lax.dot_general(qih[h], ki, _NT, preferred_element_type=F32)
            sc = sc + jnp.maximum(dd, 0.0) * wih[h]
        sc = sc * (IDX_HEADS ** -0.5)
        kpos = kb * tk + lax.broadcasted_iota(I32, (tq, tk), 1)
        qpos = i * tq + lax.broadcasted_iota(I32, (tq, tk), 0)
        key_ref[kb] = jnp.where(kpos <= qpos, _sortable_key(sc), INT_MIN)
        return carry

    lax.fori_loop(0, n_kb, a_step, 0)

    def count_where(pred):
        def body(kb, c):
            return c + jnp.where(pred(key_ref[kb]), 1.0, 0.0)
        c = lax.fori_loop(0, n_kb, body, jnp.zeros((tq, tk), F32))
        return jnp.sum(c, axis=1, keepdims=True)

    t_sel = _kth_largest_key(lambda cand: count_where(lambda keys: keys >= cand), tq, topk)
    need = topk - count_where(lambda keys: keys > t_sel)

    q = q_ref[...]
    for g in range(DSA_KV_HEADS):
        for r in range(DSA_HEADS // DSA_KV_HEADS):
            h = (DSA_HEADS // DSA_KV_HEADS) * g + r
            x = q[:, (h // 2) * LANE_V7X:(h // 2 + 1) * LANE_V7X]
            if h % 2 != g % 2:
                x = pltpu.roll(x, HEAD_DIM, 1)
            x = jnp.where(low if g % 2 == 0 else jnp.logical_not(low), x, 0.0)
            qs_ref[g, r * tq:(r + 1) * tq, :] = x.astype(BF16)
    m_ref[...] = jnp.full(m_ref.shape, NEG, F32)
    l_ref[...] = jnp.zeros(l_ref.shape, F32)
    acc_ref[...] = jnp.zeros(acc_ref.shape, F32)
    upper = _strict_upper(tk)
    reps = DSA_HEADS // DSA_KV_HEADS

    def c_step(kb, tie_count):
        start = pl.multiple_of(kb * tk, tk)
        self, tie_count = _tie_select(key_ref[kb], t_sel, need, tie_count, upper)
        sel = jnp.tile(self, (reps, 1)) > 0.5
        k = k_ref[pl.ds(start, tk), :]
        v = v_ref[pl.ds(start, tk), :]
        for g in range(DSA_KV_HEADS):
            cols = slice((g // 2) * LANE_V7X, (g // 2 + 1) * LANE_V7X)
            s = lax.dot_general(qs_ref[g], k[:, cols], _NT, preferred_element_type=F32)
            s = jnp.where(sel, s, NEG)
            _online_softmax_update(s, sel, v[:, cols], m_ref, l_ref, acc_ref, idx=g)
        return tie_count

    lax.fori_loop(0, n_kb, c_step, jnp.zeros((tq, 1), F32))

    for u in range(DSA_HEADS // 2):
        g = u // 2
        og = acc_ref[g] / l_ref[g]
        r0 = 2 * (u % 2)
        a = og[r0 * tq:(r0 + 1) * tq]
        b = og[(r0 + 1) * tq:(r0 + 2) * tq]
        if g % 2 == 0:
            tile = jnp.where(low, a, pltpu.roll(b, HEAD_DIM, 1))
        else:
            tile = jnp.where(low, pltpu.roll(a, HEAD_DIM, 1), b)
        o_ref[:, u * LANE_V7X:(u + 1) * LANE_V7X] = tile.astype(BF16)


DSA_COL_Q = 0
DSA_COL_QI = 1024
DSA_COL_K = 1536
DSA_COL_KI = 1792
DSA_ROPE_COLS = 2048
DSA_COL_V = 2048
DSA_COL_WI = 2304
DSA_COLS = 2560


def dsa_attention_prompt(pf, pb, *, batch, seq, tq, tk):
    nq = seq // tq
    topk = min(DSA_TOPK_MAX, seq // 4)
    assert tk % tq == 0 and seq % tk == 0
    reps = DSA_HEADS // DSA_KV_HEADS
    kern = functools.partial(_dsa_prompt_kernel, tq, tk, topk)
    return pl.pallas_call(
        kern,
        grid=(batch, nq),
        in_specs=[
            pl.BlockSpec((tq, 1024), lambda b, i: (b * nq + i, DSA_COL_Q // 1024)),
            pl.BlockSpec((tq, 512), lambda b, i: (b * nq + i, DSA_COL_QI // 512)),
            pl.BlockSpec((tq, LANE_V7X), lambda b, i: (b * nq + i, DSA_COL_WI // LANE_V7X)),
            pl.BlockSpec((seq, 256), lambda b, i: (b, DSA_COL_K // 256)),
            pl.BlockSpec((seq, 256), lambda b, i: (b, DSA_COL_V // 256)),
            pl.BlockSpec((seq, LANE_V7X), lambda b, i: (b, DSA_COL_KI // LANE_V7X)),
        ],
        out_specs=pl.BlockSpec((tq, D_MODEL), lambda b, i: (b * nq + i, 0)),
        out_shape=jax.ShapeDtypeStruct((batch * seq, D_MODEL), BF16),
        scratch_shapes=[
            pltpu.VMEM((seq // tk, tq, tk), I32),
            pltpu.VMEM((DSA_KV_HEADS, reps * tq, LANE_V7X), BF16),
            pltpu.VMEM((DSA_KV_HEADS, reps * tq, LANE_V7X), F32),
            pltpu.VMEM((DSA_KV_HEADS, reps * tq, 1), F32),
            pltpu.VMEM((DSA_KV_HEADS, reps * tq, 1), F32),
        ],
        compiler_params=_cparams(("parallel", "arbitrary")),
        name="dsa_attention_prompt",
    )(pf, pf, pf, pb, pb, pb)


def _dsa_select_kernel(pages_per_step, n_page_steps, topk, pt_ref, *refs):
    P = pages_per_step
    qi_ref, w_ref = refs[0], refs[1]
    kp = refs[2:2 + P]
    kinew_ref, o_ref, key_ref = refs[2 + P:]
    s_id = pl.program_id(1)
    n_chunks = n_page_steps * P + 1
    nq = SUBLANE_V7X

    def chunk_score(kpage):
        dd = lax.dot_general(qi_ref[0], kpage.astype(BF16), _NT, preferred_element_type=F32)
        dd = jnp.maximum(dd, 0.0) * w_ref[0]
        sc = dd[0:nq]
        for h in range(1, IDX_HEADS):
            sc = sc + dd[h * nq:(h + 1) * nq]
        return sc * (IDX_HEADS ** -0.5)

    @pl.when(s_id < n_page_steps)
    def _():
        for p in range(P):
            key_ref[s_id * P + p] = _sortable_key(chunk_score(kp[p][0]))

    @pl.when(s_id == n_page_steps)
    def _():
        row = lax.broadcasted_iota(I32, (nq, PAGE), 0)
        col = lax.broadcasted_iota(I32, (nq, PAGE), 1)
        key_ref[n_chunks - 1] = jnp.where(col <= row, _sortable_key(chunk_score(kinew_ref[0])), INT_MIN)

        def count_where(pred):
            c = jnp.zeros((nq, PAGE), F32)
            for ch in range(n_chunks):
                c = c + jnp.where(pred(key_ref[ch]), 1.0, 0.0)
            return jnp.sum(c, axis=1, keepdims=True)

        t_sel = _kth_largest_key(lambda cand: count_where(lambda keys: keys >= cand), nq, topk)
        need = topk - count_where(lambda keys: keys > t_sel)
        upper = _strict_upper(PAGE)
        tie_count = jnp.zeros((nq, 1), F32)
        for ch in range(n_chunks):
            self, tie_count = _tie_select(key_ref[ch], t_sel, need, tie_count, upper)
            o_ref[0, ch] = self


def dsa_select_sample(qi_stack, w_stack, pool_kidx, layer, page_table, ki_new, *, pages_per_step):
    n_seq, n_pages = page_table.shape
    P = pages_per_step
    nps = n_pages // P
    topk = min(DSA_TOPK_MAX, (n_pages * PAGE + SUBLANE_V7X) // 4)
    pt = (page_table + layer).reshape(-1).astype(I32)

    def page_map(p):
        def f(b, s, pt_ref):
            return (pt_ref[b * n_pages + jnp.minimum(s, nps - 1) * P + p], 0, 0)
        return f

    rows = IDX_HEADS * SUBLANE_V7X
    in_specs = [pl.BlockSpec((1, rows, HEAD_DIM), lambda b, s, pt_ref: (b, 0, 0)),
                pl.BlockSpec((1, rows, PAGE), lambda b, s, pt_ref: (b, 0, 0))]
    in_specs += [pl.BlockSpec((1, PAGE, HEAD_DIM), page_map(p)) for p in range(P)]
    in_specs += [pl.BlockSpec((1, PAGE, HEAD_DIM), lambda b, s, pt_ref: (b, 0, 0))]
    kern = functools.partial(_dsa_select_kernel, P, nps, topk)
    return pl.pallas_call(
        kern,
        grid_spec=pltpu.PrefetchScalarGridSpec(
            num_scalar_prefetch=1,
            grid=(n_seq, nps + 1),
            in_specs=in_specs,
            out_specs=pl.BlockSpec((1, n_pages + 1, SUBLANE_V7X, PAGE), lambda b, s, pt_ref: (b, 0, 0, 0)),
            scratch_shapes=[pltpu.VMEM((n_pages + 1, SUBLANE_V7X, PAGE), I32)],
        ),
        out_shape=jax.ShapeDtypeStruct((n_seq, n_pages + 1, SUBLANE_V7X, PAGE), F32),
        compiler_params=_cparams(("parallel", "arbitrary")),
        name="dsa_select_sample",
    )(pt, qi_stack, w_stack, *([pool_kidx] * P), ki_new)


ROUTER_COLS = LANE_V7X


def _router_kernel(x_ref, g_ref, w_ref, b_ref, xn_ref, info_ref, cnt_ref, carry_ref):
    i = pl.program_id(0)

    @pl.when(i == 0)
    def _():
        carry_ref[...] = jnp.zeros(carry_ref.shape, F32)

    x = x_ref[...]
    xn = x * lax.rsqrt(jnp.mean(x * x, axis=-1, keepdims=True) + NORM_EPS) * g_ref[...]
    xn_ref[...] = xn
    lg = jnp.dot(xn, w_ref[...], preferred_element_type=F32, precision=lax.Precision.HIGHEST) + b_ref[...]
    tm = lg.shape[0]
    lane = lax.broadcasted_iota(I32, lg.shape, 1)
    lanef = lane.astype(F32)
    big = float(ROUTER_COLS)

    grp_lanes = lane < 4
    zg = jnp.where(grp_lanes, lg, NEG)
    mg = jnp.max(zg, axis=1, keepdims=True)
    sg = jnp.sum(jnp.where(grp_lanes, jnp.exp(zg - mg), 0.0), axis=1, keepdims=True)
    g_gate = 1.0 / sg
    grp = jnp.min(jnp.where(grp_lanes & (zg == mg), lanef, big), axis=1, keepdims=True)

    in_grp = (lane >= 4) & (lane < 4 + N_EXPERTS) & (((lane - 4) >> 2).astype(F32) == grp)
    ze = jnp.where(in_grp, lg, NEG)
    l1 = jnp.max(ze, axis=1, keepdims=True)
    i1 = jnp.min(jnp.where(in_grp & (ze == l1), lanef, big), axis=1, keepdims=True)
    rest = in_grp & (lanef != i1)
    ze2 = jnp.where(rest, lg, NEG)
    l2 = jnp.max(ze2, axis=1, keepdims=True)
    i2 = jnp.min(jnp.where(rest & (ze2 == l2), lanef, big), axis=1, keepdims=True)
    e21 = jnp.exp(l2 - l1)
    gate0 = g_gate / (1.0 + e21)
    gate1 = g_gate * e21 / (1.0 + e21)
    e0 = i1 - 4.0
    e1 = i2 - 4.0

    oh0 = lanef == e0
    oh1 = lanef == e1
    ohs = jnp.where(oh0 | oh1, 1.0, 0.0)
    lower = jnp.where(lax.broadcasted_iota(I32, (tm, tm), 1) < lax.broadcasted_iota(I32, (tm, tm), 0), 1.0, 0.0)
    cnt = jnp.dot(lower.astype(BF16), ohs.astype(BF16), preferred_element_type=F32) + carry_ref[0:1, :]
    rank0 = jnp.sum(jnp.where(oh0, cnt, 0.0), axis=1, keepdims=True)
    rank1 = jnp.sum(jnp.where(oh1, cnt, 0.0), axis=1, keepdims=True)
    carry_ref[0:1, :] = carry_ref[0:1, :] + jnp.sum(ohs, axis=0, keepdims=True)

    info = jnp.zeros(lg.shape, F32)
    for idx, val in enumerate((e0, e1, gate0, gate1, rank0, rank1)):
        info = jnp.where(lane == idx, val, info)
    info_ref[...] = info

    @pl.when(i == pl.num_programs(0) - 1)
    def _():
        cnt_ref[...] = carry_ref[...]


def moe_router(x, g, w_router, b_router, *, tm):
    n, d = x.shape
    return pl.pallas_call(
        _router_kernel,
        grid=(n // tm,),
        in_specs=[
            pl.BlockSpec((tm, d), lambda i: (i, 0)),
            pl.BlockSpec((1, d), lambda i: (0, 0)),
            pl.BlockSpec((d, ROUTER_COLS), lambda i: (0, 0)),
            pl.BlockSpec((1, ROUTER_COLS), lambda i: (0, 0)),
        ],
        out_specs=[
            pl.BlockSpec((tm, d), lambda i: (i, 0)),
            pl.BlockSpec((tm, ROUTER_COLS), lambda i: (i, 0)),
            pl.BlockSpec((SUBLANE_V7X, ROUTER_COLS), lambda i: (0, 0)),
        ],
        out_shape=[
            jax.ShapeDtypeStruct((n, d), F32),
            jax.ShapeDtypeStruct((n, ROUTER_COLS), F32),
            jax.ShapeDtypeStruct((SUBLANE_V7X, ROUTER_COLS), F32),
        ],
        scratch_shapes=[pltpu.VMEM((SUBLANE_V7X, ROUTER_COLS), F32)],
        compiler_params=_cparams(("arbitrary",)),
        name="moe_router",
    )(x, g.reshape(1, d), w_router, b_router)


def _expert_kernel(be_ref, nv_ref, nu_ref, gcur_ref, gnext_ref, sidx_ref, x_hbm, wup_ref, wdn_ref, y_hbm,
                   xbuf, ybuf, gsem, ssem):
    del be_ref
    i = pl.program_id(0)
    n_used = nu_ref[0]
    slot = i % 2
    mb = xbuf.shape[1]

    def start_gather(idx_ref, dst_slot):
        def body(r, carry):
            tok = idx_ref[0, 0, r]
            pltpu.make_async_copy(x_hbm.at[pl.ds(tok, 1), :], xbuf.at[dst_slot, pl.ds(r, 1), :],
                                  gsem.at[dst_slot]).start()
            return carry
        lax.fori_loop(0, mb, body, 0)

    def scatter_copy(src_slot, r, row):
        return pltpu.make_async_copy(ybuf.at[src_slot, pl.ds(r, 1), :], y_hbm.at[pl.ds(row, 1), :],
                                     ssem.at[src_slot])

    @pl.when(i == 0)
    def _():
        start_gather(gcur_ref, 0)

    @pl.when(i + 1 < n_used)
    def _():
        start_gather(gnext_ref, 1 - slot)

    @pl.when(i < n_used)
    def _():
        pltpu.make_async_copy(x_hbm.at[pl.ds(0, mb), :], xbuf.at[slot], gsem.at[slot]).wait()
        x = xbuf[slot].astype(BF16)
        h = jnp.dot(x, wup_ref[0], preferred_element_type=F32)
        a = (jax.nn.silu(h[:, :EXPERT_FF]) * h[:, EXPERT_FF:]).astype(BF16)
        ybuf[slot] = jnp.dot(a, wdn_ref[0], preferred_element_type=F32)

        def issue(r, carry):
            scatter_copy(slot, r, sidx_ref[0, 0, r]).start()
            return carry
        lax.fori_loop(0, nv_ref[i], issue, 0)

        def wait_rows(src_slot, count):
            def body(r, carry):
                scatter_copy(src_slot, 0, 0).wait()
                return carry
            lax.fori_loop(0, count, body, 0)

        @pl.when(i >= 1)
        def _():
            wait_rows(1 - slot, nv_ref[jnp.maximum(i - 1, 0)])

        @pl.when(i == n_used - 1)
        def _():
            wait_rows(slot, nv_ref[i])


def moe_experts(xn, w_up, w_down, block_e, n_valid, n_used, gidx, sidx, *, n_slots):
    n_blocks = block_e.shape[0]
    mb = MOE_ROWS
    d = D_MODEL
    smem_blk = lambda f: pl.BlockSpec((1, 1, mb), f, memory_space=pltpu.SMEM)
    return pl.pallas_call(
        _expert_kernel,
        grid_spec=pltpu.PrefetchScalarGridSpec(
            num_scalar_prefetch=3,
            grid=(n_blocks,),
            in_specs=[
                smem_blk(lambda i, be, nv, nu: (i, 0, 0)),
                smem_blk(lambda i, be, nv, nu: (jnp.minimum(i + 1, n_blocks - 1), 0, 0)),
                smem_blk(lambda i, be, nv, nu: (i, 0, 0)),
                pl.BlockSpec(memory_space=pl.ANY),
                pl.BlockSpec((1, d, 2 * EXPERT_FF), lambda i, be, nv, nu: (be[i], 0, 0)),
                pl.BlockSpec((1, EXPERT_FF, d), lambda i, be, nv, nu: (be[i], 0, 0)),
            ],
            out_specs=pl.BlockSpec(memory_space=pl.ANY),
            scratch_shapes=[
                pltpu.VMEM((2, mb, d), F32),
                pltpu.VMEM((2, mb, d), F32),
                pltpu.SemaphoreType.DMA((2,)),
                pltpu.SemaphoreType.DMA((2,)),
            ],
        ),
        out_shape=jax.ShapeDtypeStruct((n_slots, d), F32),
        compiler_params=_cparams(("arbitrary",)),
        name="moe_experts",
    )(block_e, n_valid, n_used, gidx.reshape(n_blocks, 1, mb), gidx.reshape(n_blocks, 1, mb),
      sidx.reshape(n_blocks, 1, mb), xn, w_up, w_down)


def _combine_kernel(final_norm, x_ref, y_ref, info_ref, g_ref, o_ref):
    d = D_MODEL
    info = info_ref[...]
    x = x_ref[...] + info[:, 2:3] * y_ref[:, :d] + info[:, 3:4] * y_ref[:, d:]
    if final_norm:
        x = x * lax.rsqrt(jnp.mean(x * x, axis=-1, keepdims=True) + NORM_EPS) * g_ref[...]
    o_ref[...] = x


def moe_combine(x, y2, info, g_final, *, final_norm, tm):
    n, d = x.shape
    return pl.pallas_call(
        functools.partial(_combine_kernel, final_norm),
        grid=(n // tm,),
        in_specs=[
            pl.BlockSpec((tm, d), lambda i: (i, 0)),
            pl.BlockSpec((tm, 2 * d), lambda i: (i, 0)),
            pl.BlockSpec((tm, ROUTER_COLS), lambda i: (i, 0)),
            pl.BlockSpec((1, d), lambda i: (0, 0)),
        ],
        out_specs=pl.BlockSpec((tm, d), lambda i: (i, 0)),
        out_shape=jax.ShapeDtypeStruct((n, d), F32),
        compiler_params=_cparams(("parallel",)),
        name="moe_combine",
    )(x, y2, info, g_final.reshape(1, d))


def moe_layer(x, g_ffn, w_group, b_group, w_expert, b_expert, w_up, w_down, g_final, *, final_norm, tm):
    n, d = x.shape
    mb = MOE_ROWS
    w_router = jnp.zeros((d, ROUTER_COLS), F32).at[:, :4].set(w_group).at[:, 4:4 + N_EXPERTS].set(w_expert)
    b_router = jnp.zeros((1, ROUTER_COLS), F32).at[0, :4].set(b_group).at[0, 4:4 + N_EXPERTS].set(b_expert)
    xn, info, cnt = moe_router(x, g_ffn, w_router, b_router, tm=256)

    e01 = info[:, 0:2].astype(I32)
    rank01 = info[:, 4:6].astype(I32)
    counts = cnt[0, :N_EXPERTS].astype(I32)
    padded = (counts + mb - 1) // mb * mb
    pend = jnp.cumsum(padded)
    pstart = pend - padded
    dest = pstart[e01] + rank01
    n_blocks = (2 * n + N_EXPERTS * (mb - 1) + mb - 1) // mb
    tok = jnp.arange(n, dtype=I32)
    gidx = jnp.zeros((n_blocks * mb,), I32).at[dest[:, 0]].set(tok).at[dest[:, 1]].set(tok)
    sidx = jnp.zeros((n_blocks * mb,), I32).at[dest[:, 0]].set(2 * tok).at[dest[:, 1]].set(2 * tok + 1)
    blk_start = jnp.arange(n_blocks, dtype=I32) * mb
    block_e = jnp.minimum(jnp.searchsorted(pend, blk_start, side="right"), N_EXPERTS - 1).astype(I32)
    n_valid = jnp.clip(pstart[block_e] + counts[block_e] - blk_start, 0, mb).astype(I32)
    n_used = (pend[-1] // mb).astype(I32).reshape(1)

    y = moe_experts(xn, w_up, w_down, block_e, n_valid, n_used, gidx, sidx, n_slots=2 * n)
    return moe_combine(x, y.reshape(n, 2 * d), info, g_final, final_norm=final_norm, tm=256)


def _rope_tables(pos):
    inv = ROPE_THETA ** (-jnp.arange(0, HEAD_DIM, 2, dtype=F32) / HEAD_DIM)
    ang = pos.astype(F32)[:, None] * inv[None, :]
    cos = jnp.tile(jnp.cos(ang), (1, 2 * LANE_V7X // HEAD_DIM))
    sin = jnp.sin(ang)
    sin = jnp.tile(jnp.concatenate([-sin, sin], axis=1), (1, LANE_V7X // HEAD_DIM))
    return cos, sin


def _largest_divisor(n, candidates):
    for c in candidates:
        if n % c == 0:
            return c
    raise ValueError(f"no tile in {candidates} divides {n}")


def _pad_rows(a, rows):
    return jnp.pad(a, ((0, 0), (0, rows - a.shape[1]), (0, 0)))


def kernel(x_prompt, x_sample, cache_diff_k, cache_diff_v, state_lru_h, state_lru_conv, cache_dsa_k, cache_dsa_v,
           cache_dsa_kidx, page_table, norm_mix, norm_ffn, norm_final, diff_w_in, diff_lambda, diff_subln,
           diff_w_out, lru_w_in, lru_conv_w, lru_conv_b, lru_w_r, lru_b_r, lru_w_i, lru_b_i, lru_a, lru_w_out,
           dsa_w_in, dsa_w_out, moe_w_group, moe_b_group, moe_w_expert, moe_b_expert, moe_w_up, moe_w_down):
    batch, seq, d = x_prompt.shape
    nseq, dseq, _ = x_sample.shape
    assert dseq == SUBLANE_V7X and d == D_MODEL
    np_, ns_ = batch * seq, nseq * dseq
    n = np_ + ns_
    depth = norm_mix.shape[0]
    n_pool = cache_diff_k.shape[1]
    tm = _largest_divisor(n, (1280, 640, 256))
    tm_res = _largest_divisor(n, (640, 256))
    n_pages = page_table.shape[1]
    past = n_pages * PAGE

    x = jnp.concatenate([x_prompt.reshape(np_, d), x_sample.reshape(ns_, d)], axis=0)
    pos = jnp.concatenate([jnp.tile(jnp.arange(seq, dtype=I32), batch),
                           jnp.tile(past + jnp.arange(dseq, dtype=I32), nseq)])
    cos, sin = _rope_tables(pos)
    scale = HEAD_DIM ** -0.5

    outs = {k: [] for k in ("dk_p", "dv_p", "dk_s", "dv_s", "lh_p", "lc_p", "lh_s", "lc_s",
                            "sk_p", "sv_p", "si_p", "sk_s", "sv_s", "si_s")}

    for i in range(depth):
        kind, j = i % 3, i // 3
        if kind == 0:
            lam_init = 0.8 - 0.6 * math.exp(-0.3 * i)
            lp = diff_lambda[j]
            lam = jnp.exp(jnp.sum(lp[0] * lp[1])) - jnp.exp(jnp.sum(lp[2] * lp[3])) + lam_init
            w = jnp.concatenate([diff_w_in[j][:, :d] * scale, diff_w_in[j][:, d:]], axis=1).astype(BF16)
            pf, pb = norm_matmul(x, norm_mix[i], w, cos, sin, n_rope_cols=2 * d, tm=tm, tn=512)
            k_all, v_all = pf[:, d:2 * d], pf[:, 2 * d:]
            outs["dk_p"].append(k_all[:np_].reshape(batch, seq, 2 * DIFF_HEADS, HEAD_DIM))
            outs["dv_p"].append(v_all[:np_].reshape(batch, seq, DIFF_HEADS, 2 * HEAD_DIM))
            outs["dk_s"].append(k_all[np_:].reshape(nseq, dseq, 2 * DIFF_HEADS, HEAD_DIM))
            outs["dv_s"].append(v_all[np_:].reshape(nseq, dseq, DIFF_HEADS, 2 * HEAD_DIM))
            ap = diff_attention_prompt(pb, lam, diff_subln[j], 1.0 - lam_init, batch=batch, seq=seq, tq=256, tk=256)
            qs = pb[np_:, :d].reshape(nseq, dseq, 2 * DIFF_HEADS, HEAD_DIM).transpose(0, 2, 1, 3)
            eye = jnp.eye(2 * DIFF_HEADS, dtype=BF16)
            q_bd = (qs[:, :, :, None, :] * eye[None, :, None, :, None]).reshape(nseq, ROWS_S, d)
            a_s = paged_attention(
                "diff", q_bd, cache_diff_k.reshape(-1, PAGE, d), cache_diff_v.reshape(-1, PAGE, d), j * n_pool,
                page_table, _pad_rows(k_all[np_:].reshape(nseq, dseq, d), PAGE),
                _pad_rows(v_all[np_:].reshape(nseq, dseq, d), PAGE), pages_per_step=_largest_divisor(n_pages, (4, 2, 1)),
                lam=lam, subln=diff_subln[j], out_scale=1.0 - lam_init)
            att = jnp.concatenate([ap, a_s.reshape(ns_, d).astype(BF16)], axis=0)
            x = matmul_residual(att, diff_w_out[j].astype(BF16), x, tm=tm_res)
        elif kind == 1:
            pf, _ = norm_matmul(x, norm_mix[i], lru_w_in[j].astype(BF16), cos, sin, n_rope_cols=0, tm=tm, tn=512)
            la = 8.0 * jax.nn.log_sigmoid(lru_a[j])
            wts = (lru_conv_w[j], lru_conv_b[j], lru_w_r[j], lru_b_r[j], lru_w_i[j], lru_b_i[j], la)
            yp, hp, cp = lru_core(pf[:np_].reshape(batch, seq, 2 * d), jnp.zeros((batch, 1, d), F32),
                                  jnp.zeros((batch, SUBLANE_V7X, d), F32), *wts, gb=1, tm=256)
            c0 = jnp.pad(state_lru_conv[j], ((0, 0), (SUBLANE_V7X - (CONV_W - 1), 0), (0, 0)))
            ys, hs, cs = lru_core(pf[np_:].reshape(nseq, dseq, 2 * d), state_lru_h[j][:, None, :], c0, *wts,
                                  gb=nseq, tm=dseq)
            outs["lh_p"].append(hp[:, 0])
            outs["lc_p"].append(cp[:, SUBLANE_V7X - (CONV_W - 1):])
            outs["lh_s"].append(hs[:, 0])
            outs["lc_s"].append(cs[:, SUBLANE_V7X - (CONV_W - 1):])
            yh = jnp.concatenate([yp.reshape(np_, d), ys.reshape(ns_, d)], axis=0)
            x = matmul_residual(yh, lru_w_out[j].astype(BF16), x, tm=tm_res)
        else:
            wj = dsa_w_in[j]
            o_q, o_k, o_v, o_qi, o_ki, o_wi = 0, 1024, 1280, 1536, 2048, 2112
            w = jnp.zeros((d, DSA_COLS), F32)
            w = w.at[:, DSA_COL_Q:DSA_COL_Q + 1024].set(wj[:, o_q:o_k] * scale)
            w = w.at[:, DSA_COL_QI:DSA_COL_QI + 512].set(wj[:, o_qi:o_ki] * scale)
            w = w.at[:, DSA_COL_K:DSA_COL_K + 256].set(wj[:, o_k:o_v])
            w = w.at[:, DSA_COL_KI:DSA_COL_KI + 64].set(wj[:, o_ki:o_wi])
            w = w.at[:, DSA_COL_V:DSA_COL_V + 256].set(wj[:, o_v:o_qi])
            w = w.at[:, DSA_COL_WI:DSA_COL_WI + IDX_HEADS].set(wj[:, o_wi:])
            pf, pb = norm_matmul(x, norm_mix[i], w.astype(BF16), cos, sin, n_rope_cols=DSA_ROPE_COLS, tm=tm, tn=512)
            k_all = pf[:, DSA_COL_K:DSA_COL_K + 256]
            v_all = pf[:, DSA_COL_V:DSA_COL_V + 256]
            ki_all = pf[:, DSA_COL_KI:DSA_COL_KI + HEAD_DIM]
            outs["sk_p"].append(k_all[:np_].reshape(batch, seq, DSA_KV_HEADS, HEAD_DIM))
            outs["sv_p"].append(v_all[:np_].reshape(batch, seq, DSA_KV_HEADS, HEAD_DIM))
            outs["si_p"].append(ki_all[:np_].reshape(batch, seq, HEAD_DIM))
            outs["sk_s"].append(k_all[np_:].reshape(nseq, dseq, DSA_KV_HEADS, HEAD_DIM))
            outs["sv_s"].append(v_all[np_:].reshape(nseq, dseq, DSA_KV_HEADS, HEAD_DIM))
            outs["si_s"].append(ki_all[np_:].reshape(nseq, dseq, HEAD_DIM))
            ap = dsa_attention_prompt(pf, pb, batch=batch, seq=seq, tq=128, tk=256)
            ps = pf[np_:]
            qi_stack = ps[:, DSA_COL_QI:DSA_COL_QI + 512].reshape(nseq, dseq, IDX_HEADS, HEAD_DIM)
            qi_stack = qi_stack.transpose(0, 2, 1, 3).reshape(nseq, IDX_HEADS * dseq, HEAD_DIM).astype(BF16)
            w_stack = ps[:, DSA_COL_WI:DSA_COL_WI + IDX_HEADS].reshape(nseq, dseq, IDX_HEADS).transpose(0, 2, 1)
            w_stack = jnp.broadcast_to(w_stack.reshape(nseq, IDX_HEADS * dseq, 1), (nseq, IDX_HEADS * dseq, PAGE))
            ki_new = _pad_rows(ki_all[np_:].reshape(nseq, dseq, HEAD_DIM), PAGE)
            mask = dsa_select_sample(qi_stack, w_stack, cache_dsa_kidx.reshape(-1, PAGE, HEAD_DIM), j * n_pool,
                                     page_table, ki_new, pages_per_step=_largest_divisor(n_pages, (16, 8, 4, 2, 1)))
            dkv = DSA_KV_HEADS * HEAD_DIM
            qs = ps[:, :d].reshape(nseq, dseq, DSA_HEADS, HEAD_DIM).transpose(0, 2, 1, 3)
            head_g = jnp.arange(DSA_HEADS) // (DSA_HEADS // DSA_KV_HEADS)
            onehot = (head_g[:, None] == jnp.arange(DSA_KV_HEADS)[None, :]).astype(F32)
            q_bd = (qs[:, :, :, None, :] * onehot[None, :, None, :, None]).reshape(nseq, ROWS_S, dkv).astype(BF16)
            acc = paged_attention(
                "dsa", q_bd, cache_dsa_k.reshape(-1, PAGE, dkv), cache_dsa_v.reshape(-1, PAGE, dkv), j * n_pool,
                page_table, _pad_rows(k_all[np_:].reshape(nseq, dseq, dkv), PAGE),
                _pad_rows(v_all[np_:].reshape(nseq, dseq, dkv), PAGE),
                pages_per_step=_largest_divisor(n_pages, (8, 4, 2, 1)), mask=mask)
            acc = acc.reshape(nseq, DSA_HEADS, dseq, DSA_KV_HEADS, HEAD_DIM)
            a_s = jnp.sum(acc * onehot[None, :, None, :, None], axis=3).transpose(0, 2, 1, 3).reshape(ns_, d)
            att = jnp.concatenate([ap, a_s.astype(BF16)], axis=0)
            x = matmul_residual(att, dsa_w_out[j].astype(BF16), x, tm=tm_res)

        x = moe_layer(x, norm_ffn[i], moe_w_group[i], moe_b_group[i], moe_w_expert[i], moe_b_expert[i],
                      moe_w_up[i].astype(BF16), moe_w_down[i].astype(BF16), norm_final,
                      final_norm=(i == depth - 1), tm=tm)

    st = lambda key: jnp.stack(outs[key])
    return (x[:np_].reshape(batch, seq, d), x[np_:].reshape(nseq, dseq, d),
            st("dk_p"), st("dv_p"), st("dk_s"), st("dv_s"),
            st("lh_p"), st("lc_p"), st("lh_s"), st("lc_s"),
            st("sk_p"), st("sv_p"), st("si_p"), st("sk_s"), st("sv_s"), st("si_s"))
```

```python
import functools
import math

import jax
import jax.numpy as jnp
from jax import lax
from jax.experimental import pallas as pl
from jax.experimental.pallas import tpu as pltpu

F32 = jnp.float32
BF16 = jnp.bfloat16
I32 = jnp.int32

LANE_V7X = 128
SUBLANE_V7X = 8
VMEM_LIMIT_BYTES = 48 * 1024 * 1024

D_MODEL = 1024
PAST_LEN = 8192
PAGE = 128
ROPE_THETA = 10000.0
NORM_EPS = 1e-6
SUBLN_EPS = 1e-5
HEAD_DIM = 64
DIFF_HEADS = 8
LRU_BLOCKS = 4
LRU_BLOCK_W = D_MODEL // LRU_BLOCKS
CONV_W = 4
DSA_HEADS = 16
DSA_KV_HEADS = 4
IDX_HEADS = 8
DSA_TOPK_MAX = 256
N_EXPERTS = 16
EXPERT_FF = 512
MOE_ROWS = 256

NEG = -1e30
INT_MIN = -(2 ** 31)

_NT = (((1,), (1,)), ((), ()))
_BNN = (((2,), (1,)), ((0,), (0,)))
_BNT = (((2,), (2,)), ((0,), (0,)))


def _cparams(sem):
    return pltpu.CompilerParams(dimension_semantics=sem, vmem_limit_bytes=VMEM_LIMIT_BYTES)


def _norm_matmul_kernel(n_rope_steps, x_ref, g_ref, w_ref, cos_ref, sin_ref, of_ref, ob_ref, xn_ref):
    j = pl.program_id(1)

    @pl.when(j == 0)
    def _():
        x = x_ref[...]
        inv = lax.rsqrt(jnp.mean(x * x, axis=-1, keepdims=True) + NORM_EPS)
        xn_ref[...] = (x * inv * g_ref[...]).astype(BF16)

    acc = jnp.dot(xn_ref[...], w_ref[...], preferred_element_type=F32)
    tn = acc.shape[1]

    def store(val):
        of_ref[...] = val
        ob_ref[...] = val.astype(BF16)

    if n_rope_steps == 0:
        store(acc)
        return

    @pl.when(j < n_rope_steps)
    def _():
        cos = cos_ref[...]
        sin = sin_ref[...]
        lane = lax.broadcasted_iota(I32, cos.shape, 1)
        first = (lane % HEAD_DIM) < (HEAD_DIM // 2)
        parts = []
        for c in range(tn // LANE_V7X):
            a = acc[:, c * LANE_V7X:(c + 1) * LANE_V7X]
            rot = jnp.where(first, pltpu.roll(a, LANE_V7X - HEAD_DIM // 2, 1), pltpu.roll(a, HEAD_DIM // 2, 1))
            parts.append(a * cos + rot * sin)
        store(jnp.concatenate(parts, axis=1))

    @pl.when(j >= n_rope_steps)
    def _():
        store(acc)


def norm_matmul(x, g, w, cos, sin, *, n_rope_cols, tm, tn):
    n, d = x.shape
    m = w.shape[1]
    assert n % tm == 0 and m % tn == 0 and n_rope_cols % tn == 0
    kern = functools.partial(_norm_matmul_kernel, n_rope_cols // tn)
    return pl.pallas_call(
        kern,
        grid=(n // tm, m // tn),
        in_specs=[
            pl.BlockSpec((tm, d), lambda i, j: (i, 0)),
            pl.BlockSpec((1, d), lambda i, j: (0, 0)),
            pl.BlockSpec((d, tn), lambda i, j: (0, j)),
            pl.BlockSpec((tm, LANE_V7X), lambda i, j: (i, 0)),
            pl.BlockSpec((tm, LANE_V7X), lambda i, j: (i, 0)),
        ],
        out_specs=[
            pl.BlockSpec((tm, tn), lambda i, j: (i, j)),
            pl.BlockSpec((tm, tn), lambda i, j: (i, j)),
        ],
        out_shape=[jax.ShapeDtypeStruct((n, m), F32), jax.ShapeDtypeStruct((n, m), BF16)],
        scratch_shapes=[pltpu.VMEM((tm, d), BF16)],
        compiler_params=_cparams(("parallel", "arbitrary")),
        name="norm_matmul",
    )(x, g.reshape(1, d), w, cos, sin)


def _matmul_res_kernel(a_ref, w_ref, r_ref, o_ref):
    o_ref[...] = r_ref[...] + jnp.dot(a_ref[...], w_ref[...], preferred_element_type=F32)


def matmul_residual(a, w, res, *, tm):
    n, k = a.shape
    m = w.shape[1]
    assert n % tm == 0
    return pl.pallas_call(
        _matmul_res_kernel,
        grid=(n // tm,),
        in_specs=[
            pl.BlockSpec((tm, k), lambda i: (i, 0)),
            pl.BlockSpec((k, m), lambda i: (0, 0)),
            pl.BlockSpec((tm, m), lambda i: (i, 0)),
        ],
        out_specs=pl.BlockSpec((tm, m), lambda i: (i, 0)),
        out_shape=jax.ShapeDtypeStruct((n, m), F32),
        compiler_params=_cparams(("parallel",)),
        name="matmul_residual",
    )(a, w, res)


def _flash_update_t(s_t, sel_t, v_t, m_ref, l_ref, acc_ref, idx=None):
    ix = (Ellipsis,) if idx is None else (idx,)
    m_prev = m_ref[ix]
    m_new = jnp.maximum(m_prev, jnp.max(s_t, axis=0, keepdims=True))
    alpha = jnp.exp(m_prev - m_new)
    p = jnp.exp(s_t - m_new)
    if sel_t is not None:
        p = jnp.where(sel_t, p, 0.0)
    l_ref[ix] = alpha * l_ref[ix] + jnp.sum(p, axis=0, keepdims=True)
    acc_ref[ix] = alpha * acc_ref[ix] + jnp.dot(v_t, p.astype(BF16), preferred_element_type=F32)
    m_ref[ix] = m_new


def _flash_update_b(s, sel, pv_fn, m_ref, l_ref, acc_ref):
    m_prev = m_ref[...]
    m_new = jnp.maximum(m_prev, jnp.max(s, axis=-1, keepdims=True))
    alpha = jnp.exp(m_prev - m_new)
    p = jnp.exp(s - m_new)
    if sel is not None:
        p = jnp.where(sel, p, 0.0)
    l_ref[...] = alpha * l_ref[...] + jnp.sum(p, axis=-1, keepdims=True)
    acc_ref[...] = alpha * acc_ref[...] + pv_fn(p.astype(BF16))
    m_ref[...] = m_new


def _diff_prompt_kernel(tq, tk, out_scale, lam_ref, q_ref, k_ref, vt_ref, sub_ref, o_ref, acc_ref, m_ref, l_ref):
    i = pl.program_id(2)
    q = q_ref[...]
    lane = lax.broadcasted_iota(I32, q.shape, 1)
    zero = jnp.zeros_like(q)
    qq = jnp.concatenate([jnp.where(lane < HEAD_DIM, q, zero), jnp.where(lane >= HEAD_DIM, q, zero)], axis=0)
    m_ref[...] = jnp.full(m_ref.shape, NEG, F32)
    l_ref[...] = jnp.zeros(l_ref.shape, F32)
    acc_ref[...] = jnp.zeros(acc_ref.shape, F32)

    def step(kb, masked):
        start = pl.multiple_of(kb * tk, tk)
        k = k_ref[pl.ds(start, tk), :]
        s_t = lax.dot_general(k, qq, _NT, preferred_element_type=F32)
        sel = None
        if masked:
            kpos = kb * tk + lax.broadcasted_iota(I32, s_t.shape, 0)
            qpos = i * tq + (lax.broadcasted_iota(I32, s_t.shape, 1) % tq)
            sel = kpos <= qpos
            s_t = jnp.where(sel, s_t, NEG)
        _flash_update_t(s_t, sel, vt_ref[0, 0, kb], m_ref, l_ref, acc_ref)

    n_full = (i * tq) // tk

    def body(kb, carry):
        step(kb, False)
        return carry

    lax.fori_loop(0, n_full, body, 0)
    for d in range(tq // tk):
        step(n_full + d, True)

    acc = acc_ref[...]
    l = l_ref[...]
    o_t = acc[:, :tq] / l[:, :tq] - lam_ref[0:1, 0:1] * (acc[:, tq:] / l[:, tq:])
    o_t = o_t * lax.rsqrt(jnp.mean(o_t * o_t, axis=0, keepdims=True) + SUBLN_EPS) * sub_ref[...] * out_scale
    o_ref[...] = o_t.T.astype(BF16)


def diff_attention_prompt(qkv, vt, lam, subln, out_scale, *, batch, seq, tq, tk):
    nq = seq // tq
    nkb = seq // tk
    assert tq % tk == 0
    kern = functools.partial(_diff_prompt_kernel, tq, tk, out_scale)
    return pl.pallas_call(
        kern,
        grid=(batch, DIFF_HEADS, nq),
        in_specs=[
            pl.BlockSpec((1, LANE_V7X), lambda b, h, i: (0, 0)),
            pl.BlockSpec((tq, LANE_V7X), lambda b, h, i: (b * nq + i, h)),
            pl.BlockSpec((seq, LANE_V7X), lambda b, h, i: (b, DIFF_HEADS + h)),
            pl.BlockSpec((1, 1, nkb, LANE_V7X, tk), lambda b, h, i: (b, h, 0, 0, 0)),
            pl.BlockSpec((LANE_V7X, 1), lambda b, h, i: (0, 0)),
        ],
        out_specs=pl.BlockSpec((tq, LANE_V7X), lambda b, h, i: (b * nq + i, h)),
        out_shape=jax.ShapeDtypeStruct((batch * seq, D_MODEL), BF16),
        scratch_shapes=[
            pltpu.VMEM((LANE_V7X, 2 * tq), F32),
            pltpu.VMEM((1, 2 * tq), F32),
            pltpu.VMEM((1, 2 * tq), F32),
        ],
        compiler_params=_cparams(("parallel", "parallel", "arbitrary")),
        name="diff_attention_prompt",
    )(jnp.full((1, LANE_V7X), lam, F32), qkv, qkv, vt, subln.reshape(LANE_V7X, 1))


def _page_maps(n_pages, pages_per_step, n_page_steps, ndim):
    def page_map(p):
        def f(b, s, pt_ref):
            page = pt_ref[b * n_pages + jnp.minimum(s, n_page_steps - 1) * pages_per_step + p]
            return (page,) + (0,) * (ndim - 1)
        return f
    return [page_map(p) for p in range(pages_per_step)]


def _diff_sample_kernel(P, nps, out_scale, pt_ref, q_ref, *refs):
    kp, vp = refs[:P], refs[P:2 * P]
    knew_ref, vnew_ref, lam_ref, sub_ref, o_ref, acc_ref, m_ref, l_ref = refs[2 * P:]
    s_id = pl.program_id(1)
    nq = SUBLANE_V7X

    @pl.when(s_id == 0)
    def _():
        m_ref[...] = jnp.full(m_ref.shape, NEG, F32)
        l_ref[...] = jnp.zeros(l_ref.shape, F32)
        acc_ref[...] = jnp.zeros(acc_ref.shape, F32)

    q = q_ref[0]

    def attend(kt_ref, v_ref, sel):
        kt = kt_ref[0].reshape(DIFF_HEADS, 2 * HEAD_DIM, PAGE).astype(BF16)
        s = lax.dot_general(q, kt, _BNN, preferred_element_type=F32)
        if sel is not None:
            s = jnp.where(sel, s, NEG)

        def pv(p):
            v = jnp.stack([v_ref[0, pl.ds(h, PAGE, stride=DIFF_HEADS), :] for h in range(DIFF_HEADS)])
            return lax.dot_general(p, v.astype(BF16), _BNN, preferred_element_type=F32)

        _flash_update_b(s, sel, pv, m_ref, l_ref, acc_ref)

    @pl.when(s_id < nps)
    def _():
        for p in range(P):
            attend(kp[p], vp[p], None)

    @pl.when(s_id == nps)
    def _():
        shape = (DIFF_HEADS, 2 * nq, PAGE)
        sel = lax.broadcasted_iota(I32, shape, 2) <= (lax.broadcasted_iota(I32, shape, 1) % nq)
        attend(knew_ref, vnew_ref, sel)
        o = acc_ref[...] / l_ref[...]
        oh = o[:, :nq, :] - lam_ref[0:1, 0:1] * o[:, nq:, :]
        oh = oh * lax.rsqrt(jnp.mean(oh * oh, axis=-1, keepdims=True) + SUBLN_EPS) * sub_ref[...] * out_scale
        for h in range(DIFF_HEADS):
            o_ref[0, :, h * LANE_V7X:(h + 1) * LANE_V7X] = oh[h]


def diff_attention_sample(q8, pool_kt, pool_v, layer, page_table, knew_t, vnew, lam, subln, out_scale, *,
                          pages_per_step):
    n_seq, n_pages = page_table.shape
    P = pages_per_step
    nps = n_pages // P
    pt = (page_table + layer).reshape(-1).astype(I32)
    d = D_MODEL
    fixed = lambda nd: (lambda b, s, pt_ref: (b,) + (0,) * (nd - 1))
    in_specs = [pl.BlockSpec((1, DIFF_HEADS, 2 * SUBLANE_V7X, LANE_V7X), fixed(4))]
    in_specs += [pl.BlockSpec((1, 2 * DIFF_HEADS, HEAD_DIM, PAGE), m) for m in _page_maps(n_pages, P, nps, 4)]
    in_specs += [pl.BlockSpec((1, PAGE * DIFF_HEADS, LANE_V7X), m) for m in _page_maps(n_pages, P, nps, 3)]
    in_specs += [pl.BlockSpec((1, 2 * DIFF_HEADS, HEAD_DIM, PAGE), fixed(4)),
                 pl.BlockSpec((1, PAGE * DIFF_HEADS, LANE_V7X), fixed(3)),
                 pl.BlockSpec((1, LANE_V7X), lambda b, s, pt_ref: (0, 0)),
                 pl.BlockSpec((1, LANE_V7X), lambda b, s, pt_ref: (0, 0))]
    kern = functools.partial(_diff_sample_kernel, P, nps, out_scale)
    stat = pltpu.VMEM((DIFF_HEADS, 2 * SUBLANE_V7X, 1), F32)
    return pl.pallas_call(
        kern,
        grid_spec=pltpu.PrefetchScalarGridSpec(
            num_scalar_prefetch=1,
            grid=(n_seq, nps + 1),
            in_specs=in_specs,
            out_specs=pl.BlockSpec((1, SUBLANE_V7X, d), fixed(3)),
            scratch_shapes=[pltpu.VMEM((DIFF_HEADS, 2 * SUBLANE_V7X, LANE_V7X), F32), stat, stat],
        ),
        out_shape=jax.ShapeDtypeStruct((n_seq, SUBLANE_V7X, d), F32),
        compiler_params=_cparams(("parallel", "arbitrary")),
        name="diff_attention_sample",
    )(pt, q8, *([pool_kt] * P), *([pool_v] * P), knew_t, vnew,
      jnp.full((1, LANE_V7X), lam, F32), subln.reshape(1, LANE_V7X))


def _dsa_sample_kernel(P, nps, pt_ref, q_ref, *refs):
    kp, vp = refs[:P], refs[P:2 * P]
    knew_ref, vnew_ref, mask_ref, o_ref, acc_ref, m_ref, l_ref = refs[2 * P:]
    s_id = pl.program_id(1)
    reps = DSA_HEADS // DSA_KV_HEADS

    @pl.when(s_id == 0)
    def _():
        m_ref[...] = jnp.full(m_ref.shape, NEG, F32)
        l_ref[...] = jnp.zeros(l_ref.shape, F32)
        acc_ref[...] = jnp.zeros(acc_ref.shape, F32)

    q = q_ref[0]

    def attend(kt_ref, vt_ref, page):
        sel = jnp.tile(mask_ref[0, page], (reps, 1)) > 0.5
        sel = jnp.broadcast_to(sel[None], (DSA_KV_HEADS,) + sel.shape)
        s = lax.dot_general(q, kt_ref[0].astype(BF16), _BNN, preferred_element_type=F32)
        s = jnp.where(sel, s, NEG)
        vt = vt_ref[0].astype(BF16)
        _flash_update_b(s, sel, lambda p: lax.dot_general(p, vt, _BNT, preferred_element_type=F32),
                        m_ref, l_ref, acc_ref)

    @pl.when(s_id < nps)
    def _():
        for p in range(P):
            attend(kp[p], vp[p], s_id * P + p)

    @pl.when(s_id == nps)
    def _():
        attend(knew_ref, vnew_ref, nps * P)
        o_ref[0] = acc_ref[...] / l_ref[...]


def dsa_attention_sample(q4, pool_kt, pool_vt, layer, page_table, knew_t, vnew_t, mask, *, pages_per_step):
    n_seq, n_pages = page_table.shape
    P = pages_per_step
    nps = n_pages // P
    pt = (page_table + layer).reshape(-1).astype(I32)
    rows = (DSA_HEADS // DSA_KV_HEADS) * SUBLANE_V7X
    fixed = lambda nd: (lambda b, s, pt_ref: (b,) + (0,) * (nd - 1))
    page_blk = (1, DSA_KV_HEADS, HEAD_DIM, PAGE)
    in_specs = [pl.BlockSpec((1, DSA_KV_HEADS, rows, HEAD_DIM), fixed(4))]
    in_specs += [pl.BlockSpec(page_blk, m) for m in _page_maps(n_pages, P, nps, 4)]
    in_specs += [pl.BlockSpec(page_blk, m) for m in _page_maps(n_pages, P, nps, 4)]
    in_specs += [pl.BlockSpec(page_blk, fixed(4)), pl.BlockSpec(page_blk, fixed(4)),
                 pl.BlockSpec((1, n_pages + 1, SUBLANE_V7X, PAGE), fixed(4))]
    stat = pltpu.VMEM((DSA_KV_HEADS, rows, 1), F32)
    return pl.pallas_call(
        functools.partial(_dsa_sample_kernel, P, nps),
        grid_spec=pltpu.PrefetchScalarGridSpec(
            num_scalar_prefetch=1,
            grid=(n_seq, nps + 1),
            in_specs=in_specs,
            out_specs=pl.BlockSpec((1, DSA_KV_HEADS, rows, HEAD_DIM), fixed(4)),
            scratch_shapes=[pltpu.VMEM((DSA_KV_HEADS, rows, HEAD_DIM), F32), stat, stat],
        ),
        out_shape=jax.ShapeDtypeStruct((n_seq, DSA_KV_HEADS, rows, HEAD_DIM), F32),
        compiler_params=_cparams(("parallel", "arbitrary")),
        name="dsa_attention_sample",
    )(pt, q4, *([pool_kt] * P), *([pool_vt] * P), knew_t, vnew_t, mask)


def _shift_rows(x, prev8, j, row):
    r = pltpu.roll(x, j, 1)
    p = pltpu.roll(prev8, j, 1)
    if x.shape[1] > SUBLANE_V7X:
        p = jnp.concatenate([p, r[:, SUBLANE_V7X:]], axis=1)
    return jnp.where(row < j, p, r)


def _lru_kernel(gx_ref, h0_ref, c0_ref, cw_ref, cb_ref, wr_ref, br_ref, wi_ref, bi_ref, la_ref,
                y_ref, hout_ref, cout_ref, h_ref, c_ref):
    t_id = pl.program_id(1)

    @pl.when(t_id == 0)
    def _():
        h_ref[...] = h0_ref[...]
        c_ref[...] = c0_ref[...]

    gb, tm, _ = gx_ref.shape
    d = D_MODEL
    gate = gx_ref[:, :, :d]
    xr = gx_ref[:, :, d:]
    row = lax.broadcasted_iota(I32, (gb, tm, d), 1)
    prev8 = c_ref[...]
    u = cb_ref[...] + xr * cw_ref[CONV_W - 1]
    for j in range(1, CONV_W):
        u = u + _shift_rows(xr, prev8, j, row) * cw_ref[CONV_W - 1 - j]
    c_ref[...] = xr[:, tm - SUBLANE_V7X:, :]

    u2 = u.reshape(gb * tm, d).astype(BF16)
    rs, is_ = [], []
    for n in range(LRU_BLOCKS):
        ub = u2[:, n * LRU_BLOCK_W:(n + 1) * LRU_BLOCK_W]
        rs.append(jnp.dot(ub, wr_ref[n], preferred_element_type=F32))
        is_.append(jnp.dot(ub, wi_ref[n], preferred_element_type=F32))
    r = jax.nn.sigmoid(jnp.concatenate(rs, axis=1).reshape(gb, tm, d) + br_ref[...])
    ig = jax.nn.sigmoid(jnp.concatenate(is_, axis=1).reshape(gb, tm, d) + bi_ref[...])
    log_a = r * la_ref[...]
    a = jnp.exp(log_a)
    b = jnp.sqrt(1.0 - a * a) * ig * u

    dist = 1
    while dist < tm:
        a_sh = jnp.where(row >= dist, pltpu.roll(a, dist, 1), 1.0)
        b_sh = jnp.where(row >= dist, pltpu.roll(b, dist, 1), 0.0)
        b = b + a * b_sh
        a = a * a_sh
        dist *= 2
    hs = b + a * h_ref[...]
    h_ref[...] = hs[:, tm - 1:tm, :]
    y_ref[...] = (jax.nn.gelu(gate, approximate=True) * hs).astype(BF16)

    @pl.when(t_id == pl.num_programs(1) - 1)
    def _():
        hout_ref[...] = h_ref[...]
        cout_ref[...] = c_ref[...]


def lru_core(gx, h0, c0, conv_w, conv_b, w_r, b_r, w_i, b_i, la, *, gb, tm):
    g, t, _ = gx.shape
    d = D_MODEL
    assert g % gb == 0 and t % tm == 0 and tm % SUBLANE_V7X == 0
    vec = lambda a: a.reshape(1, 1, d)
    full3 = lambda shp: pl.BlockSpec(shp, lambda i, s: (0, 0, 0))
    return pl.pallas_call(
        _lru_kernel,
        grid=(g // gb, t // tm),
        in_specs=[
            pl.BlockSpec((gb, tm, 2 * d), lambda i, s: (i, s, 0)),
            pl.BlockSpec((gb, 1, d), lambda i, s: (i, 0, 0)),
            pl.BlockSpec((gb, SUBLANE_V7X, d), lambda i, s: (i, 0, 0)),
            full3((CONV_W, 1, d)),
            full3((1, 1, d)),
            full3((LRU_BLOCKS, LRU_BLOCK_W, LRU_BLOCK_W)),
            full3((1, 1, d)),
            full3((LRU_BLOCKS, LRU_BLOCK_W, LRU_BLOCK_W)),
            full3((1, 1, d)),
            full3((1, 1, d)),
        ],
        out_specs=[
            pl.BlockSpec((gb, tm, d), lambda i, s: (i, s, 0)),
            pl.BlockSpec((gb, 1, d), lambda i, s: (i, 0, 0)),
            pl.BlockSpec((gb, SUBLANE_V7X, d), lambda i, s: (i, 0, 0)),
        ],
        out_shape=[
            jax.ShapeDtypeStruct((g, t, d), BF16),
            jax.ShapeDtypeStruct((g, 1, d), F32),
            jax.ShapeDtypeStruct((g, SUBLANE_V7X, d), F32),
        ],
        scratch_shapes=[pltpu.VMEM((gb, 1, d), F32), pltpu.VMEM((gb, SUBLANE_V7X, d), F32)],
        compiler_params=_cparams(("parallel", "arbitrary")),
        name="lru_core",
    )(gx, h0, c0, conv_w.reshape(CONV_W, 1, d), vec(conv_b), w_r.astype(BF16), vec(b_r), w_i.astype(BF16), vec(b_i), vec(la))


def _sortable_key(score):
    bits = pltpu.bitcast(score, I32)
    key = bits ^ ((bits >> 31) & 0x7FFFFFFF)
    return jnp.where(score == 0.0, 0, key)


def _kth_largest_key(count_ge, shape, topk):
    def body(it, t_cur):
        cand = t_cur ^ lax.shift_left(jnp.int32(1), 31 - it)
        return jnp.where(count_ge(cand) >= topk, cand, t_cur)
    return lax.fori_loop(0, 32, body, jnp.full(shape, INT_MIN, I32))


def _tie_select(keys, t_sel, need, tie_count, tri, key_axis):
    eq = keys == t_sel
    eqf = jnp.where(eq, 1.0, 0.0)
    if key_axis == 1:
        pref = jnp.dot(eqf.astype(BF16), tri, preferred_element_type=F32)
    else:
        pref = jnp.dot(tri, eqf.astype(BF16), preferred_element_type=F32)
    take = jnp.where(eq, jnp.where(tie_count + pref < need, 1.0, 0.0), 0.0)
    self = jnp.where(keys > t_sel, 1.0, take)
    self = jnp.where(keys == INT_MIN, 0.0, self)
    return self, tie_count + jnp.sum(eqf, axis=key_axis, keepdims=True)


def _strict_tri(n, upper):
    r = lax.broadcasted_iota(I32, (n, n), 0)
    c = lax.broadcasted_iota(I32, (n, n), 1)
    return jnp.where((r < c) if upper else (c < r), 1.0, 0.0).astype(BF16)


def _dsa_prompt_kernel(tq, tk, topk, q_ref, qi_ref, wi_ref, k_ref, vt_ref, ki_ref, o_ref,
                       key_ref, qs_ref, acc_ref, m_ref, l_ref):
    i = pl.program_id(1)
    n_kb = (i * tq) // tk + tq // tk
    reps = DSA_HEADS // DSA_KV_HEADS
    lane = lax.broadcasted_iota(I32, (tq, LANE_V7X), 1)
    low = lane < HEAD_DIM

    qi = qi_ref[...]
    wi_t = wi_ref[...].T
    qih = []
    for h in range(IDX_HEADS):
        t = qi[:, (h // 2) * LANE_V7X:(h // 2 + 1) * LANE_V7X]
        if h % 2:
            t = pltpu.roll(t, HEAD_DIM, 1)
        qih.append(t.astype(BF16))

    def a_step(kb, carry):
        start = pl.multiple_of(kb * tk, tk)
        ki = ki_ref[pl.ds(start, tk), :]
        sc = jnp.zeros((tk, tq), F32)
        for h in range(IDX_HEADS):
            dd = lax.dot_general(ki, qih[h], _NT, preferred_element_type=F32)
            sc = sc + jnp.maximum(dd, 0.0) * wi_t[h:h + 1, :]
        sc = sc * (IDX_HEADS ** -0.5)
        kpos = kb * tk + lax.broadcasted_iota(I32, (tk, tq), 0)
        qpos = i * tq + lax.broadcasted_iota(I32, (tk, tq), 1)
        key_ref[kb] = jnp.where(kpos <= qpos, _sortable_key(sc), INT_MIN)
        return carry

    lax.fori_loop(0, n_kb, a_step, 0)

    def count_where(pred):
        def body(kb, c):
            hit = jnp.where(pred(key_ref[kb]), 1.0, 0.0)
            return c + jnp.sum(hit.reshape(tk // SUBLANE_V7X, SUBLANE_V7X, tq), axis=0)
        c = lax.fori_loop(0, n_kb, body, jnp.zeros((SUBLANE_V7X, tq), F32))
        return jnp.sum(c, axis=0, keepdims=True)

    t_sel = _kth_largest_key(lambda cand: count_where(lambda keys: keys >= cand), (1, tq), topk)
    need = topk - count_where(lambda keys: keys > t_sel)

    q = q_ref[...]
    for g in range(DSA_KV_HEADS):
        for r in range(reps):
            h = reps * g + r
            x = q[:, (h // 2) * LANE_V7X:(h // 2 + 1) * LANE_V7X]
            if h % 2 != g % 2:
                x = pltpu.roll(x, HEAD_DIM, 1)
            x = jnp.where(low if g % 2 == 0 else jnp.logical_not(low), x, 0.0)
            qs_ref[g, r * tq:(r + 1) * tq, :] = x.astype(BF16)
    m_ref[...] = jnp.full(m_ref.shape, NEG, F32)
    l_ref[...] = jnp.zeros(l_ref.shape, F32)
    acc_ref[...] = jnp.zeros(acc_ref.shape, F32)
    lower = _strict_tri(tk, upper=False)

    def c_step(kb, tie_count):
        start = pl.multiple_of(kb * tk, tk)
        self, tie_count = _tie_select(key_ref[kb], t_sel, need, tie_count, lower, 0)
        sel = jnp.tile(self, (1, reps)) > 0.5
        k = k_ref[pl.ds(start, tk), :]
        vt = vt_ref[0, kb]
        for g in range(DSA_KV_HEADS):
            pair = slice((g // 2) * LANE_V7X, (g // 2 + 1) * LANE_V7X)
            s_t = lax.dot_general(k[:, pair], qs_ref[g], _NT, preferred_element_type=F32)
            s_t = jnp.where(sel, s_t, NEG)
            _flash_update_t(s_t, sel, vt[pair, :], m_ref, l_ref, acc_ref, idx=g)
        return tie_count

    lax.fori_loop(0, n_kb, c_step, jnp.zeros((1, tq), F32))

    pieces = []
    for g in range(DSA_KV_HEADS):
        og = acc_ref[g] / l_ref[g]
        for r in range(reps):
            pieces.append(og[(g % 2) * HEAD_DIM:(g % 2 + 1) * HEAD_DIM, r * tq:(r + 1) * tq])
    o_ref[...] = jnp.concatenate(pieces, axis=0).T.astype(BF16)


DSA_COL_Q = 0
DSA_COL_QI = 1024
DSA_COL_K = 1536
DSA_COL_KI = 1792
DSA_ROPE_COLS = 2048
DSA_COL_V = 2048
DSA_COL_WI = 2304
DSA_COLS = 2560


def dsa_attention_prompt(pf, pb, vt, *, batch, seq, tq, tk):
    nq = seq // tq
    nkb = seq // tk
    topk = min(DSA_TOPK_MAX, seq // 4)
    assert tq % tk == 0 and seq % tq == 0 and tq % LANE_V7X == 0
    reps = DSA_HEADS // DSA_KV_HEADS
    dkv = DSA_KV_HEADS * HEAD_DIM
    kern = functools.partial(_dsa_prompt_kernel, tq, tk, topk)
    return pl.pallas_call(
        kern,
        grid=(batch, nq),
        in_specs=[
            pl.BlockSpec((tq, 1024), lambda b, i: (b * nq + i, DSA_COL_Q // 1024)),
            pl.BlockSpec((tq, 512), lambda b, i: (b * nq + i, DSA_COL_QI // 512)),
            pl.BlockSpec((tq, LANE_V7X), lambda b, i: (b * nq + i, DSA_COL_WI // LANE_V7X)),
            pl.BlockSpec((seq, dkv), lambda b, i: (b, DSA_COL_K // dkv)),
            pl.BlockSpec((1, nkb, dkv, tk), lambda b, i: (b, 0, 0, 0)),
            pl.BlockSpec((seq, LANE_V7X), lambda b, i: (b, DSA_COL_KI // LANE_V7X)),
        ],
        out_specs=pl.BlockSpec((tq, D_MODEL), lambda b, i: (b * nq + i, 0)),
        out_shape=jax.ShapeDtypeStruct((batch * seq, D_MODEL), BF16),
        scratch_shapes=[
            pltpu.VMEM((nkb, tk, tq), I32),
            pltpu.VMEM((DSA_KV_HEADS, reps * tq, LANE_V7X), BF16),
            pltpu.VMEM((DSA_KV_HEADS, LANE_V7X, reps * tq), F32),
            pltpu.VMEM((DSA_KV_HEADS, 1, reps * tq), F32),
            pltpu.VMEM((DSA_KV_HEADS, 1, reps * tq), F32),
        ],
        compiler_params=_cparams(("parallel", "arbitrary")),
        name="dsa_attention_prompt",
    )(pf, pf, pf, pb, vt, pb)


def _dsa_select_kernel(pages_per_step, n_page_steps, topk, pt_ref, *refs):
    P = pages_per_step
    qi_ref, w_ref = refs[0], refs[1]
    kp = refs[2:2 + P]
    kinew_ref, o_ref, key_ref = refs[2 + P:]
    s_id = pl.program_id(1)
    n_chunks = n_page_steps * P + 1
    nq = SUBLANE_V7X

    def chunk_score(kit):
        dd = jnp.dot(qi_ref[0], kit.astype(BF16), preferred_element_type=F32)
        dd = jnp.maximum(dd, 0.0) * w_ref[0]
        sc = dd[0:nq]
        for h in range(1, IDX_HEADS):
            sc = sc + dd[h * nq:(h + 1) * nq]
        return sc * (IDX_HEADS ** -0.5)

    @pl.when(s_id < n_page_steps)
    def _():
        for p in range(P):
            key_ref[s_id * P + p] = _sortable_key(chunk_score(kp[p][0]))

    @pl.when(s_id == n_page_steps)
    def _():
        row = lax.broadcasted_iota(I32, (nq, PAGE), 0)
        col = lax.broadcasted_iota(I32, (nq, PAGE), 1)
        key_ref[n_chunks - 1] = jnp.where(col <= row, _sortable_key(chunk_score(kinew_ref[0])), INT_MIN)

        def count_where(pred):
            c = jnp.zeros((nq, PAGE), F32)
            for ch in range(n_chunks):
                c = c + jnp.where(pred(key_ref[ch]), 1.0, 0.0)
            return jnp.sum(c, axis=1, keepdims=True)

        t_sel = _kth_largest_key(lambda cand: count_where(lambda keys: keys >= cand), (nq, 1), topk)
        need = topk - count_where(lambda keys: keys > t_sel)
        upper = _strict_tri(PAGE, upper=True)
        tie_count = jnp.zeros((nq, 1), F32)
        for ch in range(n_chunks):
            self, tie_count = _tie_select(key_ref[ch], t_sel, need, tie_count, upper, 1)
            o_ref[0, ch] = self


def dsa_select_sample(qi_stack, w_stack, pool_kit, layer, page_table, ki_new_t, *, pages_per_step):
    n_seq, n_pages = page_table.shape
    P = pages_per_step
    nps = n_pages // P
    topk = min(DSA_TOPK_MAX, (n_pages * PAGE + SUBLANE_V7X) // 4)
    pt = (page_table + layer).reshape(-1).astype(I32)
    rows = IDX_HEADS * SUBLANE_V7X
    fixed = lambda b, s, pt_ref: (b, 0, 0)
    in_specs = [pl.BlockSpec((1, rows, HEAD_DIM), fixed), pl.BlockSpec((1, rows, PAGE), fixed)]
    in_specs += [pl.BlockSpec((1, HEAD_DIM, PAGE), m) for m in _page_maps(n_pages, P, nps, 3)]
    in_specs += [pl.BlockSpec((1, HEAD_DIM, PAGE), fixed)]
    kern = functools.partial(_dsa_select_kernel, P, nps, topk)
    return pl.pallas_call(
        kern,
        grid_spec=pltpu.PrefetchScalarGridSpec(
            num_scalar_prefetch=1,
            grid=(n_seq, nps + 1),
            in_specs=in_specs,
            out_specs=pl.BlockSpec((1, n_pages + 1, SUBLANE_V7X, PAGE), lambda b, s, pt_ref: (b, 0, 0, 0)),
            scratch_shapes=[pltpu.VMEM((n_pages + 1, SUBLANE_V7X, PAGE), I32)],
        ),
        out_shape=jax.ShapeDtypeStruct((n_seq, n_pages + 1, SUBLANE_V7X, PAGE), F32),
        compiler_params=_cparams(("parallel", "arbitrary")),
        name="dsa_select_sample",
    )(pt, qi_stack, w_stack, *([pool_kit] * P), ki_new_t)


ROUTER_COLS = LANE_V7X
TOKEN_TILE = D_MODEL // LANE_V7X


def _store_token_tiles(ref, val, *lead):
    rows = val.shape[0]
    for c in range(TOKEN_TILE):
        ref[lead + (pl.ds(c, rows, stride=TOKEN_TILE), slice(None))] = val[:, c * LANE_V7X:(c + 1) * LANE_V7X]


def _load_token_tiles(ref, rows, *lead):
    return jnp.concatenate([ref[lead + (pl.ds(c, rows, stride=TOKEN_TILE), slice(None))]
                            for c in range(TOKEN_TILE)], axis=1)


def _router_kernel(x_ref, g_ref, w_ref, b_ref, xn_ref, info_ref, cnt_ref, carry_ref):
    i = pl.program_id(0)

    @pl.when(i == 0)
    def _():
        carry_ref[...] = jnp.zeros(carry_ref.shape, F32)

    x = x_ref[...]
    xn = x * lax.rsqrt(jnp.mean(x * x, axis=-1, keepdims=True) + NORM_EPS) * g_ref[...]
    _store_token_tiles(xn_ref, xn)
    lg = jnp.dot(xn, w_ref[...], preferred_element_type=F32, precision=lax.Precision.HIGHEST) + b_ref[...]
    tm = lg.shape[0]
    lane = lax.broadcasted_iota(I32, lg.shape, 1)
    lanef = lane.astype(F32)
    big = float(ROUTER_COLS)

    grp_lanes = lane < 4
    zg = jnp.where(grp_lanes, lg, NEG)
    mg = jnp.max(zg, axis=1, keepdims=True)
    sg = jnp.sum(jnp.where(grp_lanes, jnp.exp(zg - mg), 0.0), axis=1, keepdims=True)
    g_gate = 1.0 / sg
    grp = jnp.min(jnp.where(grp_lanes & (zg == mg), lanef, big), axis=1, keepdims=True)

    in_grp = (lane >= 4) & (lane < 4 + N_EXPERTS) & (((lane - 4) >> 2).astype(F32) == grp)
    ze = jnp.where(in_grp, lg, NEG)
    l1 = jnp.max(ze, axis=1, keepdims=True)
    i1 = jnp.min(jnp.where(in_grp & (ze == l1), lanef, big), axis=1, keepdims=True)
    rest = in_grp & (lanef != i1)
    ze2 = jnp.where(rest, lg, NEG)
    l2 = jnp.max(ze2, axis=1, keepdims=True)
    i2 = jnp.min(jnp.where(rest & (ze2 == l2), lanef, big), axis=1, keepdims=True)
    e21 = jnp.exp(l2 - l1)
    gate0 = g_gate / (1.0 + e21)
    gate1 = g_gate * e21 / (1.0 + e21)
    e0 = i1 - 4.0
    e1 = i2 - 4.0

    oh0 = lanef == e0
    oh1 = lanef == e1
    ohs = jnp.where(oh0 | oh1, 1.0, 0.0)
    lower = jnp.where(lax.broadcasted_iota(I32, (tm, tm), 1) < lax.broadcasted_iota(I32, (tm, tm), 0), 1.0, 0.0)
    cnt = jnp.dot(lower.astype(BF16), ohs.astype(BF16), preferred_element_type=F32) + carry_ref[0:1, :]
    rank0 = jnp.sum(jnp.where(oh0, cnt, 0.0), axis=1, keepdims=True)
    rank1 = jnp.sum(jnp.where(oh1, cnt, 0.0), axis=1, keepdims=True)
    carry_ref[0:1, :] = carry_ref[0:1, :] + jnp.sum(ohs, axis=0, keepdims=True)

    info = jnp.zeros(lg.shape, F32)
    for idx, val in enumerate((e0, e1, gate0, gate1, rank0, rank1)):
        info = jnp.where(lane == idx, val, info)
    info_ref[...] = info

    @pl.when(i == pl.num_programs(0) - 1)
    def _():
        cnt_ref[...] = carry_ref[...]


def moe_router(x, g, w_router, b_router, *, tm):
    n, d = x.shape
    return pl.pallas_call(
        _router_kernel,
        grid=(n // tm,),
        in_specs=[
            pl.BlockSpec((tm, d), lambda i: (i, 0)),
            pl.BlockSpec((1, d), lambda i: (0, 0)),
            pl.BlockSpec((d, ROUTER_COLS), lambda i: (0, 0)),
            pl.BlockSpec((1, ROUTER_COLS), lambda i: (0, 0)),
        ],
        out_specs=[
            pl.BlockSpec((tm * TOKEN_TILE, LANE_V7X), lambda i: (i, 0)),
            pl.BlockSpec((tm, ROUTER_COLS), lambda i: (i, 0)),
            pl.BlockSpec((SUBLANE_V7X, ROUTER_COLS), lambda i: (0, 0)),
        ],
        out_shape=[
            jax.ShapeDtypeStruct((n * TOKEN_TILE, LANE_V7X), F32),
            jax.ShapeDtypeStruct((n, ROUTER_COLS), F32),
            jax.ShapeDtypeStruct((SUBLANE_V7X, ROUTER_COLS), F32),
        ],
        scratch_shapes=[pltpu.VMEM((SUBLANE_V7X, ROUTER_COLS), F32)],
        compiler_params=_cparams(("arbitrary",)),
        name="moe_router",
    )(x, g.reshape(1, d), w_router, b_router)


def _expert_kernel(spare_row0, be_ref, nu_ref, gcur_ref, gnext_ref, sidx_ref, x_hbm, wup_ref, wdn_ref, y_hbm,
                   xbuf, ybuf, gsem, ssem):
    del be_ref
    i = pl.program_id(0)
    n_used = nu_ref[0]
    slot = i % 2
    tt = TOKEN_TILE
    mb = xbuf.shape[1] // tt

    def tile_rows(start):
        return pl.ds(pl.multiple_of(start, tt), tt)

    def start_gather(idx_ref, dst_slot):
        def body(r, carry):
            pltpu.make_async_copy(x_hbm.at[tile_rows(idx_ref[0, 0, r]), :], xbuf.at[dst_slot, tile_rows(r * tt), :],
                                  gsem.at[dst_slot]).start()
            return carry
        lax.fori_loop(0, mb, body, 0, unroll=8)

    def scatter_block(src_slot):
        return pltpu.make_async_copy(ybuf.at[src_slot], y_hbm.at[pl.ds(0, mb * tt), :], ssem.at[src_slot])

    @pl.when(i == 0)
    def _():
        ybuf[0] = jnp.zeros(ybuf.shape[1:], F32)
        for s in range(2):
            spare = pltpu.make_async_copy(ybuf.at[0], y_hbm.at[pl.ds(spare_row0 + s * mb * tt, mb * tt), :],
                                          ssem.at[0])
            spare.start()
            spare.wait()
        start_gather(gcur_ref, 0)

    @pl.when(i + 1 < n_used)
    def _():
        start_gather(gnext_ref, 1 - slot)

    @pl.when(i < n_used)
    def _():
        pltpu.make_async_copy(x_hbm.at[pl.ds(0, mb * tt), :], xbuf.at[slot], gsem.at[slot]).wait()
        x = _load_token_tiles(xbuf, mb, slot).astype(BF16)
        h = jnp.dot(x, wup_ref[0], preferred_element_type=F32)
        a = (jax.nn.silu(h[:, :EXPERT_FF]) * h[:, EXPERT_FF:]).astype(BF16)
        _store_token_tiles(ybuf, jnp.dot(a, wdn_ref[0], preferred_element_type=F32), slot)

        def issue(r, carry):
            pltpu.make_async_copy(ybuf.at[slot, tile_rows(r * tt), :], y_hbm.at[tile_rows(sidx_ref[0, 0, r]), :],
                                  ssem.at[slot]).start()
            return carry
        lax.fori_loop(0, mb, issue, 0, unroll=8)

        @pl.when(i >= 1)
        def _():
            scatter_block(1 - slot).wait()

        @pl.when(i == n_used - 1)
        def _():
            scatter_block(slot).wait()


def moe_experts(xn, w_up, w_down, block_e, n_used, gidx, sidx, *, n_rows_out):
    n_blocks = block_e.shape[0]
    mb = MOE_ROWS
    d = D_MODEL
    tt = TOKEN_TILE
    smem_blk = lambda f: pl.BlockSpec((1, 1, mb), f, memory_space=pltpu.SMEM)
    return pl.pallas_call(
        functools.partial(_expert_kernel, (n_rows_out - 2 * mb) * tt),
        grid_spec=pltpu.PrefetchScalarGridSpec(
            num_scalar_prefetch=2,
            grid=(n_blocks,),
            in_specs=[
                smem_blk(lambda i, be, nu: (i, 0, 0)),
                smem_blk(lambda i, be, nu: (jnp.minimum(i + 1, n_blocks - 1), 0, 0)),
                smem_blk(lambda i, be, nu: (i, 0, 0)),
                pl.BlockSpec(memory_space=pl.ANY),
                pl.BlockSpec((1, d, 2 * EXPERT_FF), lambda i, be, nu: (be[i], 0, 0)),
                pl.BlockSpec((1, EXPERT_FF, d), lambda i, be, nu: (be[i], 0, 0)),
            ],
            out_specs=pl.BlockSpec(memory_space=pl.ANY),
            scratch_shapes=[
                pltpu.VMEM((2, mb * tt, LANE_V7X), F32),
                pltpu.VMEM((2, mb * tt, LANE_V7X), F32),
                pltpu.SemaphoreType.DMA((2,)),
                pltpu.SemaphoreType.DMA((2,)),
            ],
        ),
        out_shape=jax.ShapeDtypeStruct((n_rows_out * tt, LANE_V7X), F32),
        compiler_params=_cparams(("arbitrary",)),
        name="moe_experts",
    )(block_e, n_used, gidx.reshape(n_blocks, 1, mb), gidx.reshape(n_blocks, 1, mb),
      sidx.reshape(n_blocks, 1, mb), xn, w_up, w_down)


def _combine_kernel(final_norm, x_ref, y0_ref, y1_ref, info_ref, g_ref, o_ref):
    info = info_ref[...]
    rows = x_ref.shape[0]
    x = (x_ref[...] + info[:, 2:3] * _load_token_tiles(y0_ref, rows)
         + info[:, 3:4] * _load_token_tiles(y1_ref, rows))
    if final_norm:
        x = x * lax.rsqrt(jnp.mean(x * x, axis=-1, keepdims=True) + NORM_EPS) * g_ref[...]
    o_ref[...] = x


def moe_combine(x, y, info, g_final, *, final_norm, tm):
    n, d = x.shape
    nt = n // tm
    return pl.pallas_call(
        functools.partial(_combine_kernel, final_norm),
        grid=(nt,),
        in_specs=[
            pl.BlockSpec((tm, d), lambda i: (i, 0)),
            pl.BlockSpec((tm * TOKEN_TILE, LANE_V7X), lambda i: (i, 0)),
            pl.BlockSpec((tm * TOKEN_TILE, LANE_V7X), lambda i: (nt + i, 0)),
            pl.BlockSpec((tm, ROUTER_COLS), lambda i: (i, 0)),
            pl.BlockSpec((1, d), lambda i: (0, 0)),
        ],
        out_specs=pl.BlockSpec((tm, d), lambda i: (i, 0)),
        out_shape=jax.ShapeDtypeStruct((n, d), F32),
        compiler_params=_cparams(("parallel",)),
        name="moe_combine",
    )(x, y, y, info, g_final.reshape(1, d))


def moe_layer(x, g_ffn, w_group, b_group, w_expert, b_expert, w_up, w_down, g_final, *, final_norm):
    n, d = x.shape
    mb = MOE_ROWS
    w_router = jnp.zeros((d, ROUTER_COLS), F32).at[:, :4].set(w_group).at[:, 4:4 + N_EXPERTS].set(w_expert)
    b_router = jnp.zeros((1, ROUTER_COLS), F32).at[0, :4].set(b_group).at[0, 4:4 + N_EXPERTS].set(b_expert)
    xn, info, cnt = moe_router(x, g_ffn, w_router, b_router, tm=256)

    e01 = info[:, 0:2].astype(I32)
    rank01 = info[:, 4:6].astype(I32)
    counts = cnt[0, :N_EXPERTS].astype(I32)
    padded = (counts + mb - 1) // mb * mb
    pend = jnp.cumsum(padded)
    pstart = pend - padded
    dest = pstart[e01] + rank01
    n_blocks = (2 * n + N_EXPERTS * (mb - 1) + mb - 1) // mb
    blk = jnp.arange(n_blocks, dtype=I32)[:, None]
    spare = 2 * n + (blk % 2) * mb + jnp.arange(mb, dtype=I32)[None, :]
    target = (jnp.arange(n, dtype=I32)[:, None] + jnp.array([0, n], I32)[None, :]).reshape(-1)
    sidx = spare.reshape(-1).at[dest.reshape(-1)].set(target)
    gidx = jnp.where(sidx < 2 * n, sidx % n, 0)
    block_e = jnp.minimum(jnp.searchsorted(pend, blk[:, 0] * mb, side="right"), N_EXPERTS - 1).astype(I32)
    n_used = (pend[-1] // mb).astype(I32).reshape(1)

    y = moe_experts(xn, w_up, w_down, block_e, n_used, gidx * TOKEN_TILE, sidx * TOKEN_TILE,
                    n_rows_out=2 * n + 2 * mb)
    return moe_combine(x, y, info, g_final, final_norm=final_norm, tm=256)


def _rope_tables(pos):
    inv = ROPE_THETA ** (-jnp.arange(0, HEAD_DIM, 2, dtype=F32) / HEAD_DIM)
    ang = pos.astype(F32)[:, None] * inv[None, :]
    cos = jnp.tile(jnp.cos(ang), (1, 2 * LANE_V7X // HEAD_DIM))
    sin = jnp.sin(ang)
    sin = jnp.tile(jnp.concatenate([-sin, sin], axis=1), (1, LANE_V7X // HEAD_DIM))
    return cos, sin


def _largest_divisor(n, candidates):
    for c in candidates:
        if n % c == 0:
            return c
    raise ValueError(f"no tile in {candidates} divides {n}")


def _pad_last(a, size):
    return jnp.pad(a, [(0, 0)] * (a.ndim - 1) + [(0, size - a.shape[-1])])


def kernel(x_prompt, x_sample, cache_diff_k, cache_diff_v, state_lru_h, state_lru_conv, cache_dsa_k, cache_dsa_v,
           cache_dsa_kidx, page_table, norm_mix, norm_ffn, norm_final, diff_w_in, diff_lambda, diff_subln,
           diff_w_out, lru_w_in, lru_conv_w, lru_conv_b, lru_w_r, lru_b_r, lru_w_i, lru_b_i, lru_a, lru_w_out,
           dsa_w_in, dsa_w_out, moe_w_group, moe_b_group, moe_w_expert, moe_b_expert, moe_w_up, moe_w_down):
    batch, seq, d = x_prompt.shape
    nseq, dseq, _ = x_sample.shape
    assert dseq == SUBLANE_V7X and d == D_MODEL
    np_, ns_ = batch * seq, nseq * dseq
    n = np_ + ns_
    depth = norm_mix.shape[0]
    n_pool = cache_diff_k.shape[1]
    tm = _largest_divisor(n, (1280, 640, 256))
    tm_res = _largest_divisor(n, (640, 256))
    n_pages = page_table.shape[1]
    past = n_pages * PAGE
    t_diff = _largest_divisor(seq, (512, 256))
    t_dsa = 256
    dkv = DSA_KV_HEADS * HEAD_DIM

    x = jnp.concatenate([x_prompt.reshape(np_, d), x_sample.reshape(ns_, d)], axis=0)
    pos = jnp.concatenate([jnp.tile(jnp.arange(seq, dtype=I32), batch),
                           jnp.tile(past + jnp.arange(dseq, dtype=I32), nseq)])
    cos, sin = _rope_tables(pos)
    scale = HEAD_DIM ** -0.5

    pool_diff_kt = cache_diff_k.transpose(0, 1, 3, 4, 2).reshape(-1, 2 * DIFF_HEADS, HEAD_DIM, PAGE)
    pool_diff_v = cache_diff_v.reshape(-1, PAGE * DIFF_HEADS, 2 * HEAD_DIM)
    pool_dsa_kt = cache_dsa_k.transpose(0, 1, 3, 4, 2).reshape(-1, DSA_KV_HEADS, HEAD_DIM, PAGE)
    pool_dsa_vt = cache_dsa_v.transpose(0, 1, 3, 4, 2).reshape(-1, DSA_KV_HEADS, HEAD_DIM, PAGE)
    pool_dsa_kit = cache_dsa_kidx.transpose(0, 1, 3, 2).reshape(-1, HEAD_DIM, PAGE)

    outs = {k: [] for k in ("dk_p", "dv_p", "dk_s", "dv_s", "lh_p", "lc_p", "lh_s", "lc_s",
                            "sk_p", "sv_p", "si_p", "sk_s", "sv_s", "si_s")}

    for i in range(depth):
        kind, j = i % 3, i // 3
        if kind == 0:
            lam_init = 0.8 - 0.6 * math.exp(-0.3 * i)
            lp = diff_lambda[j]
            lam = jnp.exp(jnp.sum(lp[0] * lp[1])) - jnp.exp(jnp.sum(lp[2] * lp[3])) + lam_init
            w = jnp.concatenate([diff_w_in[j][:, :d] * scale, diff_w_in[j][:, d:]], axis=1).astype(BF16)
            pf, pb = norm_matmul(x, norm_mix[i], w, cos, sin, n_rope_cols=2 * d, tm=tm, tn=512)
            k_all, v_all = pf[:, d:2 * d], pf[:, 2 * d:]
            outs["dk_p"].append(k_all[:np_].reshape(batch, seq, 2 * DIFF_HEADS, HEAD_DIM))
            outs["dv_p"].append(v_all[:np_].reshape(batch, seq, DIFF_HEADS, 2 * HEAD_DIM))
            outs["dk_s"].append(k_all[np_:].reshape(nseq, dseq, 2 * DIFF_HEADS, HEAD_DIM))
            outs["dv_s"].append(v_all[np_:].reshape(nseq, dseq, DIFF_HEADS, 2 * HEAD_DIM))
            vt = pb[:np_, 2 * d:].reshape(batch, seq // t_diff, t_diff, DIFF_HEADS, 2 * HEAD_DIM)
            vt = vt.transpose(0, 3, 1, 4, 2)
            ap = diff_attention_prompt(pb, vt, lam, diff_subln[j], 1.0 - lam_init, batch=batch, seq=seq,
                                       tq=t_diff, tk=t_diff)
            qs = pb[np_:, :d].reshape(nseq, dseq, DIFF_HEADS, 2, HEAD_DIM).transpose(0, 2, 3, 1, 4)
            eye2 = jnp.eye(2, dtype=BF16)
            q8 = (qs[:, :, :, :, None, :] * eye2[None, None, :, None, :, None]).reshape(
                nseq, DIFF_HEADS, 2 * dseq, 2 * HEAD_DIM)
            knew_t = _pad_last(k_all[np_:].reshape(nseq, dseq, 2 * DIFF_HEADS, HEAD_DIM).transpose(0, 2, 3, 1), PAGE)
            vnew = jnp.pad(v_all[np_:].reshape(nseq, dseq, DIFF_HEADS, 2 * HEAD_DIM),
                           ((0, 0), (0, PAGE - dseq), (0, 0), (0, 0))).reshape(nseq, PAGE * DIFF_HEADS, 2 * HEAD_DIM)
            a_s = diff_attention_sample(q8, pool_diff_kt, pool_diff_v, j * n_pool, page_table, knew_t, vnew,
                                        lam, diff_subln[j], 1.0 - lam_init,
                                        pages_per_step=_largest_divisor(n_pages, (4, 2, 1)))
            att = jnp.concatenate([ap, a_s.reshape(ns_, d).astype(BF16)], axis=0)
            x = matmul_residual(att, diff_w_out[j].astype(BF16), x, tm=tm_res)
        elif kind == 1:
            pf, _ = norm_matmul(x, norm_mix[i], lru_w_in[j].astype(BF16), cos, sin, n_rope_cols=0, tm=tm, tn=512)
            la = 8.0 * jax.nn.log_sigmoid(lru_a[j])
            wts = (lru_conv_w[j], lru_conv_b[j], lru_w_r[j], lru_b_r[j], lru_w_i[j], lru_b_i[j], la)
            yp, hp, cp = lru_core(pf[:np_].reshape(batch, seq, 2 * d), jnp.zeros((batch, 1, d), F32),
                                  jnp.zeros((batch, SUBLANE_V7X, d), F32), *wts, gb=1, tm=256)
            c0 = jnp.pad(state_lru_conv[j], ((0, 0), (SUBLANE_V7X - (CONV_W - 1), 0), (0, 0)))
            ys, hs, cs = lru_core(pf[np_:].reshape(nseq, dseq, 2 * d), state_lru_h[j][:, None, :], c0, *wts,
                                  gb=nseq, tm=dseq)
            outs["lh_p"].append(hp[:, 0])
            outs["lc_p"].append(cp[:, SUBLANE_V7X - (CONV_W - 1):])
            outs["lh_s"].append(hs[:, 0])
            outs["lc_s"].append(cs[:, SUBLANE_V7X - (CONV_W - 1):])
            yh = jnp.concatenate([yp.reshape(np_, d), ys.reshape(ns_, d)], axis=0)
            x = matmul_residual(yh, lru_w_out[j].astype(BF16), x, tm=tm_res)
        else:
            wj = dsa_w_in[j]
            o_q, o_k, o_v, o_qi, o_ki, o_wi = 0, 1024, 1280, 1536, 2048, 2112
            w = jnp.zeros((d, DSA_COLS), F32)
            w = w.at[:, DSA_COL_Q:DSA_COL_Q + 1024].set(wj[:, o_q:o_k] * scale)
            w = w.at[:, DSA_COL_QI:DSA_COL_QI + 512].set(wj[:, o_qi:o_ki] * scale)
            w = w.at[:, DSA_COL_K:DSA_COL_K + dkv].set(wj[:, o_k:o_v])
            w = w.at[:, DSA_COL_KI:DSA_COL_KI + HEAD_DIM].set(wj[:, o_ki:o_wi])
            w = w.at[:, DSA_COL_V:DSA_COL_V + dkv].set(wj[:, o_v:o_qi])
            w = w.at[:, DSA_COL_WI:DSA_COL_WI + IDX_HEADS].set(wj[:, o_wi:])
            pf, pb = norm_matmul(x, norm_mix[i], w.astype(BF16), cos, sin, n_rope_cols=DSA_ROPE_COLS, tm=tm, tn=512)
            k_all = pf[:, DSA_COL_K:DSA_COL_K + dkv]
            v_all = pf[:, DSA_COL_V:DSA_COL_V + dkv]
            ki_all = pf[:, DSA_COL_KI:DSA_COL_KI + HEAD_DIM]
            outs["sk_p"].append(k_all[:np_].reshape(batch, seq, DSA_KV_HEADS, HEAD_DIM))
            outs["sv_p"].append(v_all[:np_].reshape(batch, seq, DSA_KV_HEADS, HEAD_DIM))
            outs["si_p"].append(ki_all[:np_].reshape(batch, seq, HEAD_DIM))
            outs["sk_s"].append(k_all[np_:].reshape(nseq, dseq, DSA_KV_HEADS, HEAD_DIM))
            outs["sv_s"].append(v_all[np_:].reshape(nseq, dseq, DSA_KV_HEADS, HEAD_DIM))
            outs["si_s"].append(ki_all[np_:].reshape(nseq, dseq, HEAD_DIM))
            vt = pb[:np_, DSA_COL_V:DSA_COL_V + dkv].reshape(batch, seq // t_dsa, t_dsa, dkv).transpose(0, 1, 3, 2)
            ap = dsa_attention_prompt(pf, pb, vt, batch=batch, seq=seq, tq=t_dsa, tk=t_dsa)
            ps = pf[np_:]
            qi_stack = ps[:, DSA_COL_QI:DSA_COL_QI + 512].reshape(nseq, dseq, IDX_HEADS, HEAD_DIM)
            qi_stack = qi_stack.transpose(0, 2, 1, 3).reshape(nseq, IDX_HEADS * dseq, HEAD_DIM).astype(BF16)
            w_stack = ps[:, DSA_COL_WI:DSA_COL_WI + IDX_HEADS].reshape(nseq, dseq, IDX_HEADS).transpose(0, 2, 1)
            w_stack = jnp.broadcast_to(w_stack.reshape(nseq, IDX_HEADS * dseq, 1), (nseq, IDX_HEADS * dseq, PAGE))
            ki_new_t = _pad_last(ki_all[np_:].reshape(nseq, dseq, HEAD_DIM).transpose(0, 2, 1), PAGE)
            mask = dsa_select_sample(qi_stack, w_stack, pool_dsa_kit, j * n_pool, page_table, ki_new_t,
                                     pages_per_step=_largest_divisor(n_pages, (16, 8, 4, 2, 1)))
            q4 = ps[:, :d].reshape(nseq, dseq, DSA_HEADS, HEAD_DIM).transpose(0, 2, 1, 3)
            q4 = q4.reshape(nseq, DSA_KV_HEADS, (DSA_HEADS // DSA_KV_HEADS) * dseq, HEAD_DIM).astype(BF16)
            new_t = lambda a: _pad_last(a[np_:].reshape(nseq, dseq, DSA_KV_HEADS, HEAD_DIM).transpose(0, 2, 3, 1), PAGE)
            o4 = dsa_attention_sample(q4, pool_dsa_kt, pool_dsa_vt, j * n_pool, page_table, new_t(k_all),
                                      new_t(v_all), mask, pages_per_step=_largest_divisor(n_pages, (8, 4, 2, 1)))
            a_s = o4.reshape(nseq, DSA_HEADS, dseq, HEAD_DIM).transpose(0, 2, 1, 3).reshape(ns_, d)
            att = jnp.concatenate([ap, a_s.astype(BF16)], axis=0)
            x = matmul_residual(att, dsa_w_out[j].astype(BF16), x, tm=tm_res)

        x = moe_layer(x, norm_ffn[i], moe_w_group[i], moe_b_group[i], moe_w_expert[i], moe_b_expert[i],
                      moe_w_up[i].astype(BF16), moe_w_down[i].astype(BF16), norm_final,
                      final_norm=(i == depth - 1))

    st = lambda key: jnp.stack(outs[key])
    return (x[:np_].reshape(batch, seq, d), x[np_:].reshape(nseq, dseq, d),
            st("dk_p"), st("dv_p"), st("dk_s"), st("dv_s"),
            st("lh_p"), st("lc_p"), st("lh_s"), st("lc_s"),
            st("sk_p"), st("sv_p"), st("si_p"), st("sk_s"), st("sv_s"), st("si_s"))
```

```python
import functools
import math

import jax
import jax.numpy as jnp
from jax import lax
from jax.experimental import pallas as pl
from jax.experimental.pallas import tpu as pltpu

F32 = jnp.float32
BF16 = jnp.bfloat16
I32 = jnp.int32

LANE_V7X = 128
SUBLANE_V7X = 8
VMEM_LIMIT_BYTES = 48 * 1024 * 1024

D_MODEL = 1024
PAST_LEN = 8192
PAGE = 128
ROPE_THETA = 10000.0
NORM_EPS = 1e-6
SUBLN_EPS = 1e-5
HEAD_DIM = 64
DIFF_HEADS = 8
LRU_BLOCKS = 4
LRU_BLOCK_W = D_MODEL // LRU_BLOCKS
CONV_W = 4
DSA_HEADS = 16
DSA_KV_HEADS = 4
IDX_HEADS = 8
DSA_TOPK_MAX = 256
N_EXPERTS = 16
EXPERT_FF = 512
MOE_ROWS = 256

NEG = -1e30
INT_MIN = -(2 ** 31)

_NT = (((1,), (1,)), ((), ()))
_BNN = (((2,), (1,)), ((0,), (0,)))
_BNT = (((2,), (2,)), ((0,), (0,)))


def _cparams(sem):
    return pltpu.CompilerParams(dimension_semantics=sem, vmem_limit_bytes=VMEM_LIMIT_BYTES)


def _norm_matmul_kernel(n_rope_steps, x_ref, g_ref, w_ref, cos_ref, sin_ref, of_ref, ob_ref, xn_ref):
    j = pl.program_id(1)

    @pl.when(j == 0)
    def _():
        x = x_ref[...]
        inv = lax.rsqrt(jnp.mean(x * x, axis=-1, keepdims=True) + NORM_EPS)
        xn_ref[...] = (x * inv * g_ref[...]).astype(BF16)

    acc = jnp.dot(xn_ref[...], w_ref[...], preferred_element_type=F32)
    tn = acc.shape[1]

    def store(val):
        of_ref[...] = val
        ob_ref[...] = val.astype(BF16)

    if n_rope_steps == 0:
        store(acc)
        return

    @pl.when(j < n_rope_steps)
    def _():
        cos = cos_ref[...]
        sin = sin_ref[...]
        lane = lax.broadcasted_iota(I32, cos.shape, 1)
        first = (lane % HEAD_DIM) < (HEAD_DIM // 2)
        parts = []
        for c in range(tn // LANE_V7X):
            a = acc[:, c * LANE_V7X:(c + 1) * LANE_V7X]
            rot = jnp.where(first, pltpu.roll(a, LANE_V7X - HEAD_DIM // 2, 1), pltpu.roll(a, HEAD_DIM // 2, 1))
            parts.append(a * cos + rot * sin)
        store(jnp.concatenate(parts, axis=1))

    @pl.when(j >= n_rope_steps)
    def _():
        store(acc)


def norm_matmul(x, g, w, cos, sin, *, n_rope_cols, tm, tn):
    n, d = x.shape
    m = w.shape[1]
    assert n % tm == 0 and m % tn == 0 and n_rope_cols % tn == 0
    kern = functools.partial(_norm_matmul_kernel, n_rope_cols // tn)
    return pl.pallas_call(
        kern,
        grid=(n // tm, m // tn),
        in_specs=[
            pl.BlockSpec((tm, d), lambda i, j: (i, 0)),
            pl.BlockSpec((1, d), lambda i, j: (0, 0)),
            pl.BlockSpec((d, tn), lambda i, j: (0, j)),
            pl.BlockSpec((tm, LANE_V7X), lambda i, j: (i, 0)),
            pl.BlockSpec((tm, LANE_V7X), lambda i, j: (i, 0)),
        ],
        out_specs=[
            pl.BlockSpec((tm, tn), lambda i, j: (i, j)),
            pl.BlockSpec((tm, tn), lambda i, j: (i, j)),
        ],
        out_shape=[jax.ShapeDtypeStruct((n, m), F32), jax.ShapeDtypeStruct((n, m), BF16)],
        scratch_shapes=[pltpu.VMEM((tm, d), BF16)],
        compiler_params=_cparams(("parallel", "arbitrary")),
        name="norm_matmul",
    )(x, g.reshape(1, d), w, cos, sin)


def _matmul_res_kernel(a_ref, w_ref, r_ref, o_ref):
    o_ref[...] = r_ref[...] + jnp.dot(a_ref[...], w_ref[...], preferred_element_type=F32)


def matmul_residual(a, w, res, *, tm):
    n, k = a.shape
    m = w.shape[1]
    assert n % tm == 0
    return pl.pallas_call(
        _matmul_res_kernel,
        grid=(n // tm,),
        in_specs=[
            pl.BlockSpec((tm, k), lambda i: (i, 0)),
            pl.BlockSpec((k, m), lambda i: (0, 0)),
            pl.BlockSpec((tm, m), lambda i: (i, 0)),
        ],
        out_specs=pl.BlockSpec((tm, m), lambda i: (i, 0)),
        out_shape=jax.ShapeDtypeStruct((n, m), F32),
        compiler_params=_cparams(("parallel",)),
        name="matmul_residual",
    )(a, w, res)


def _flash_update_t(s_t, sel_t, v_t, m_ref, l_ref, acc_ref, idx=None):
    ix = (Ellipsis,) if idx is None else (idx,)
    m_prev = m_ref[ix]
    m_new = jnp.maximum(m_prev, jnp.max(s_t, axis=0, keepdims=True))
    alpha = jnp.exp(m_prev - m_new)
    p = jnp.exp(s_t - m_new)
    if sel_t is not None:
        p = jnp.where(sel_t, p, 0.0)
    l_ref[ix] = alpha * l_ref[ix] + jnp.sum(p, axis=0, keepdims=True)
    acc_ref[ix] = alpha * acc_ref[ix] + jnp.dot(v_t, p.astype(BF16), preferred_element_type=F32)
    m_ref[ix] = m_new


def _flash_update_b(s, sel, pv_fn, m_ref, l_ref, acc_ref):
    m_prev = m_ref[...]
    m_new = jnp.maximum(m_prev, jnp.max(s, axis=-1, keepdims=True))
    alpha = jnp.exp(m_prev - m_new)
    p = jnp.exp(s - m_new)
    if sel is not None:
        p = jnp.where(sel, p, 0.0)
    l_ref[...] = alpha * l_ref[...] + jnp.sum(p, axis=-1, keepdims=True)
    acc_ref[...] = alpha * acc_ref[...] + pv_fn(p.astype(BF16))
    m_ref[...] = m_new


def _diff_prompt_kernel(tq, tk, out_scale, lam_ref, q_ref, k_ref, vt_ref, sub_ref, o_ref, acc_ref, m_ref, l_ref):
    i = pl.program_id(2)
    q = q_ref[...]
    lane = lax.broadcasted_iota(I32, q.shape, 1)
    zero = jnp.zeros_like(q)
    qq = jnp.concatenate([jnp.where(lane < HEAD_DIM, q, zero), jnp.where(lane >= HEAD_DIM, q, zero)], axis=0)
    m_ref[...] = jnp.full(m_ref.shape, NEG, F32)
    l_ref[...] = jnp.zeros(l_ref.shape, F32)
    acc_ref[...] = jnp.zeros(acc_ref.shape, F32)

    def step(kb, masked):
        start = pl.multiple_of(kb * tk, tk)
        k = k_ref[pl.ds(start, tk), :]
        s_t = lax.dot_general(k, qq, _NT, preferred_element_type=F32)
        sel = None
        if masked:
            kpos = kb * tk + lax.broadcasted_iota(I32, s_t.shape, 0)
            qpos = i * tq + (lax.broadcasted_iota(I32, s_t.shape, 1) % tq)
            sel = kpos <= qpos
            s_t = jnp.where(sel, s_t, NEG)
        _flash_update_t(s_t, sel, vt_ref[0, 0, kb], m_ref, l_ref, acc_ref)

    n_full = (i * tq) // tk

    def body(kb, carry):
        step(kb, False)
        return carry

    lax.fori_loop(0, n_full, body, 0)
    for d in range(tq // tk):
        step(n_full + d, True)

    acc = acc_ref[...]
    l = l_ref[...]
    o_t = acc[:, :tq] / l[:, :tq] - lam_ref[0:1, 0:1] * (acc[:, tq:] / l[:, tq:])
    o_t = o_t * lax.rsqrt(jnp.mean(o_t * o_t, axis=0, keepdims=True) + SUBLN_EPS) * sub_ref[...] * out_scale
    o_ref[...] = o_t.T.astype(BF16)


def diff_attention_prompt(qkv, vt, lam, subln, out_scale, *, batch, seq, tq, tk):
    nq = seq // tq
    nkb = seq // tk
    assert tq % tk == 0
    kern = functools.partial(_diff_prompt_kernel, tq, tk, out_scale)
    return pl.pallas_call(
        kern,
        grid=(batch, DIFF_HEADS, nq),
        in_specs=[
            pl.BlockSpec((1, LANE_V7X), lambda b, h, i: (0, 0)),
            pl.BlockSpec((tq, LANE_V7X), lambda b, h, i: (b * nq + i, h)),
            pl.BlockSpec((seq, LANE_V7X), lambda b, h, i: (b, DIFF_HEADS + h)),
            pl.BlockSpec((1, 1, nkb, LANE_V7X, tk), lambda b, h, i: (b, h, 0, 0, 0)),
            pl.BlockSpec((LANE_V7X, 1), lambda b, h, i: (0, 0)),
        ],
        out_specs=pl.BlockSpec((tq, LANE_V7X), lambda b, h, i: (b * nq + i, h)),
        out_shape=jax.ShapeDtypeStruct((batch * seq, D_MODEL), BF16),
        scratch_shapes=[
            pltpu.VMEM((LANE_V7X, 2 * tq), F32),
            pltpu.VMEM((1, 2 * tq), F32),
            pltpu.VMEM((1, 2 * tq), F32),
        ],
        compiler_params=_cparams(("parallel", "parallel", "arbitrary")),
        name="diff_attention_prompt",
    )(jnp.full((1, LANE_V7X), lam, F32), qkv, qkv, vt, subln.reshape(LANE_V7X, 1))


def _page_maps(n_pages, pages_per_step, n_page_steps, ndim):
    def page_map(p):
        def f(b, s, pt_ref):
            page = pt_ref[b * n_pages + jnp.minimum(s, n_page_steps - 1) * pages_per_step + p]
            return (page,) + (0,) * (ndim - 1)
        return f
    return [page_map(p) for p in range(pages_per_step)]


def _diff_sample_kernel(P, nps, out_scale, pt_ref, q_ref, *refs):
    kp, vp = refs[:P], refs[P:2 * P]
    knew_ref, vnew_ref, lam_ref, sub_ref, o_ref, acc_ref, m_ref, l_ref = refs[2 * P:]
    s_id = pl.program_id(1)
    nq = SUBLANE_V7X

    @pl.when(s_id == 0)
    def _():
        m_ref[...] = jnp.full(m_ref.shape, NEG, F32)
        l_ref[...] = jnp.zeros(l_ref.shape, F32)
        acc_ref[...] = jnp.zeros(acc_ref.shape, F32)

    q = q_ref[0]

    def attend(kt_refs, v_refs, sel):
        kt = jnp.concatenate([r[0].reshape(DIFF_HEADS, 2 * HEAD_DIM, PAGE).astype(BF16) for r in kt_refs], axis=2)
        s = lax.dot_general(q, kt, _BNN, preferred_element_type=F32)
        if sel is not None:
            s = jnp.where(sel, s, NEG)

        def pv(p):
            v = jnp.stack([
                jnp.concatenate([r[0, pl.ds(h, PAGE, stride=DIFF_HEADS), :].astype(BF16) for r in v_refs], axis=0)
                for h in range(DIFF_HEADS)])
            return lax.dot_general(p, v, _BNN, preferred_element_type=F32)

        _flash_update_b(s, sel, pv, m_ref, l_ref, acc_ref)

    @pl.when(s_id < nps)
    def _():
        attend(kp, vp, None)

    @pl.when(s_id == nps)
    def _():
        shape = (DIFF_HEADS, 2 * nq, PAGE)
        sel = lax.broadcasted_iota(I32, shape, 2) <= (lax.broadcasted_iota(I32, shape, 1) % nq)
        attend([knew_ref], [vnew_ref], sel)
        o = acc_ref[...] / l_ref[...]
        oh = o[:, :nq, :] - lam_ref[0:1, 0:1] * o[:, nq:, :]
        oh = oh * lax.rsqrt(jnp.mean(oh * oh, axis=-1, keepdims=True) + SUBLN_EPS) * sub_ref[...] * out_scale
        for h in range(DIFF_HEADS):
            o_ref[0, :, h * LANE_V7X:(h + 1) * LANE_V7X] = oh[h]


def diff_attention_sample(q8, pool_kt, pool_v, layer, page_table, knew_t, vnew, lam, subln, out_scale, *,
                          pages_per_step):
    n_seq, n_pages = page_table.shape
    P = pages_per_step
    nps = n_pages // P
    pt = (page_table + layer).reshape(-1).astype(I32)
    d = D_MODEL
    fixed = lambda nd: (lambda b, s, pt_ref: (b,) + (0,) * (nd - 1))
    in_specs = [pl.BlockSpec((1, DIFF_HEADS, 2 * SUBLANE_V7X, LANE_V7X), fixed(4))]
    in_specs += [pl.BlockSpec((1, 2 * DIFF_HEADS, HEAD_DIM, PAGE), m) for m in _page_maps(n_pages, P, nps, 4)]
    in_specs += [pl.BlockSpec((1, PAGE * DIFF_HEADS, LANE_V7X), m) for m in _page_maps(n_pages, P, nps, 3)]
    in_specs += [pl.BlockSpec((1, 2 * DIFF_HEADS, HEAD_DIM, PAGE), fixed(4)),
                 pl.BlockSpec((1, PAGE * DIFF_HEADS, LANE_V7X), fixed(3)),
                 pl.BlockSpec((1, LANE_V7X), lambda b, s, pt_ref: (0, 0)),
                 pl.BlockSpec((1, LANE_V7X), lambda b, s, pt_ref: (0, 0))]
    kern = functools.partial(_diff_sample_kernel, P, nps, out_scale)
    stat = pltpu.VMEM((DIFF_HEADS, 2 * SUBLANE_V7X, 1), F32)
    return pl.pallas_call(
        kern,
        grid_spec=pltpu.PrefetchScalarGridSpec(
            num_scalar_prefetch=1,
            grid=(n_seq, nps + 1),
            in_specs=in_specs,
            out_specs=pl.BlockSpec((1, SUBLANE_V7X, d), fixed(3)),
            scratch_shapes=[pltpu.VMEM((DIFF_HEADS, 2 * SUBLANE_V7X, LANE_V7X), F32), stat, stat],
        ),
        out_shape=jax.ShapeDtypeStruct((n_seq, SUBLANE_V7X, d), F32),
        compiler_params=_cparams(("parallel", "arbitrary")),
        name="diff_attention_sample",
    )(pt, q8, *([pool_kt] * P), *([pool_v] * P), knew_t, vnew,
      jnp.full((1, LANE_V7X), lam, F32), subln.reshape(1, LANE_V7X))


def _dsa_sample_kernel(P, nps, pt_ref, q_ref, *refs):
    kp, vp = refs[:P], refs[P:2 * P]
    knew_ref, vnew_ref, mask_ref, o_ref, acc_ref, m_ref, l_ref = refs[2 * P:]
    s_id = pl.program_id(1)
    reps = DSA_HEADS // DSA_KV_HEADS

    @pl.when(s_id == 0)
    def _():
        m_ref[...] = jnp.full(m_ref.shape, NEG, F32)
        l_ref[...] = jnp.zeros(l_ref.shape, F32)
        acc_ref[...] = jnp.zeros(acc_ref.shape, F32)

    q = q_ref[0]

    def attend(kt_refs, vt_refs, first_page):
        mask = jnp.concatenate([mask_ref[0, first_page + p] for p in range(len(kt_refs))], axis=1)
        sel = jnp.tile(mask, (reps, 1)) > 0.5
        sel = jnp.broadcast_to(sel[None], (DSA_KV_HEADS,) + sel.shape)
        kt = jnp.concatenate([r[0].astype(BF16) for r in kt_refs], axis=2)
        vt = jnp.concatenate([r[0].astype(BF16) for r in vt_refs], axis=2)
        s = lax.dot_general(q, kt, _BNN, preferred_element_type=F32)
        s = jnp.where(sel, s, NEG)
        _flash_update_b(s, sel, lambda p: lax.dot_general(p, vt, _BNT, preferred_element_type=F32),
                        m_ref, l_ref, acc_ref)

    @pl.when(s_id < nps)
    def _():
        attend(kp, vp, s_id * P)

    @pl.when(s_id == nps)
    def _():
        attend([knew_ref], [vnew_ref], nps * P)
        o_ref[0] = acc_ref[...] / l_ref[...]


def dsa_attention_sample(q4, pool_kt, pool_vt, layer, page_table, knew_t, vnew_t, mask, *, pages_per_step):
    n_seq, n_pages = page_table.shape
    P = pages_per_step
    nps = n_pages // P
    pt = (page_table + layer).reshape(-1).astype(I32)
    rows = (DSA_HEADS // DSA_KV_HEADS) * SUBLANE_V7X
    fixed = lambda nd: (lambda b, s, pt_ref: (b,) + (0,) * (nd - 1))
    page_blk = (1, DSA_KV_HEADS, HEAD_DIM, PAGE)
    in_specs = [pl.BlockSpec((1, DSA_KV_HEADS, rows, HEAD_DIM), fixed(4))]
    in_specs += [pl.BlockSpec(page_blk, m) for m in _page_maps(n_pages, P, nps, 4)]
    in_specs += [pl.BlockSpec(page_blk, m) for m in _page_maps(n_pages, P, nps, 4)]
    in_specs += [pl.BlockSpec(page_blk, fixed(4)), pl.BlockSpec(page_blk, fixed(4)),
                 pl.BlockSpec((1, n_pages + 1, SUBLANE_V7X, PAGE), fixed(4))]
    stat = pltpu.VMEM((DSA_KV_HEADS, rows, 1), F32)
    return pl.pallas_call(
        functools.partial(_dsa_sample_kernel, P, nps),
        grid_spec=pltpu.PrefetchScalarGridSpec(
            num_scalar_prefetch=1,
            grid=(n_seq, nps + 1),
            in_specs=in_specs,
            out_specs=pl.BlockSpec((1, DSA_KV_HEADS, rows, HEAD_DIM), fixed(4)),
            scratch_shapes=[pltpu.VMEM((DSA_KV_HEADS, rows, HEAD_DIM), F32), stat, stat],
        ),
        out_shape=jax.ShapeDtypeStruct((n_seq, DSA_KV_HEADS, rows, HEAD_DIM), F32),
        compiler_params=_cparams(("parallel", "arbitrary")),
        name="dsa_attention_sample",
    )(pt, q4, *([pool_kt] * P), *([pool_vt] * P), knew_t, vnew_t, mask)


def _shift_rows(x, prev8, j, row):
    r = pltpu.roll(x, j, 1)
    p = pltpu.roll(prev8, j, 1)
    if x.shape[1] > SUBLANE_V7X:
        p = jnp.concatenate([p, r[:, SUBLANE_V7X:]], axis=1)
    return jnp.where(row < j, p, r)


def _lru_kernel(gx_ref, h0_ref, c0_ref, cw_ref, cb_ref, wr_ref, br_ref, wi_ref, bi_ref, la_ref,
                y_ref, hout_ref, cout_ref, h_ref, c_ref):
    t_id = pl.program_id(1)

    @pl.when(t_id == 0)
    def _():
        h_ref[...] = h0_ref[...]
        c_ref[...] = c0_ref[...]

    gb, tm, _ = gx_ref.shape
    d = D_MODEL
    gate = gx_ref[:, :, :d]
    xr = gx_ref[:, :, d:]
    row = lax.broadcasted_iota(I32, (gb, tm, d), 1)
    prev8 = c_ref[...]
    u = cb_ref[...] + xr * cw_ref[CONV_W - 1]
    for j in range(1, CONV_W):
        u = u + _shift_rows(xr, prev8, j, row) * cw_ref[CONV_W - 1 - j]
    c_ref[...] = xr[:, tm - SUBLANE_V7X:, :]

    u2 = u.reshape(gb * tm, d).astype(BF16)
    rs, is_ = [], []
    for n in range(LRU_BLOCKS):
        ub = u2[:, n * LRU_BLOCK_W:(n + 1) * LRU_BLOCK_W]
        rs.append(jnp.dot(ub, wr_ref[n], preferred_element_type=F32))
        is_.append(jnp.dot(ub, wi_ref[n], preferred_element_type=F32))
    r = jax.nn.sigmoid(jnp.concatenate(rs, axis=1).reshape(gb, tm, d) + br_ref[...])
    ig = jax.nn.sigmoid(jnp.concatenate(is_, axis=1).reshape(gb, tm, d) + bi_ref[...])
    log_a = r * la_ref[...]
    a = jnp.exp(log_a)
    b = jnp.sqrt(1.0 - a * a) * ig * u

    dist = 1
    while dist < tm:
        a_sh = jnp.where(row >= dist, pltpu.roll(a, dist, 1), 1.0)
        b_sh = jnp.where(row >= dist, pltpu.roll(b, dist, 1), 0.0)
        b = b + a * b_sh
        a = a * a_sh
        dist *= 2
    hs = b + a * h_ref[...]
    h_ref[...] = hs[:, tm - 1:tm, :]
    y_ref[...] = (jax.nn.gelu(gate, approximate=True) * hs).astype(BF16)

    @pl.when(t_id == pl.num_programs(1) - 1)
    def _():
        hout_ref[...] = h_ref[...]
        cout_ref[...] = c_ref[...]


def lru_core(gx, h0, c0, conv_w, conv_b, w_r, b_r, w_i, b_i, la, *, gb, tm):
    g, t, _ = gx.shape
    d = D_MODEL
    assert g % gb == 0 and t % tm == 0 and tm % SUBLANE_V7X == 0
    vec = lambda a: a.reshape(1, 1, d)
    full3 = lambda shp: pl.BlockSpec(shp, lambda i, s: (0, 0, 0))
    return pl.pallas_call(
        _lru_kernel,
        grid=(g // gb, t // tm),
        in_specs=[
            pl.BlockSpec((gb, tm, 2 * d), lambda i, s: (i, s, 0)),
            pl.BlockSpec((gb, 1, d), lambda i, s: (i, 0, 0)),
            pl.BlockSpec((gb, SUBLANE_V7X, d), lambda i, s: (i, 0, 0)),
            full3((CONV_W, 1, d)),
            full3((1, 1, d)),
            full3((LRU_BLOCKS, LRU_BLOCK_W, LRU_BLOCK_W)),
            full3((1, 1, d)),
            full3((LRU_BLOCKS, LRU_BLOCK_W, LRU_BLOCK_W)),
            full3((1, 1, d)),
            full3((1, 1, d)),
        ],
        out_specs=[
            pl.BlockSpec((gb, tm, d), lambda i, s: (i, s, 0)),
            pl.BlockSpec((gb, 1, d), lambda i, s: (i, 0, 0)),
            pl.BlockSpec((gb, SUBLANE_V7X, d), lambda i, s: (i, 0, 0)),
        ],
        out_shape=[
            jax.ShapeDtypeStruct((g, t, d), BF16),
            jax.ShapeDtypeStruct((g, 1, d), F32),
            jax.ShapeDtypeStruct((g, SUBLANE_V7X, d), F32),
        ],
        scratch_shapes=[pltpu.VMEM((gb, 1, d), F32), pltpu.VMEM((gb, SUBLANE_V7X, d), F32)],
        compiler_params=_cparams(("parallel", "arbitrary")),
        name="lru_core",
    )(gx, h0, c0, conv_w.reshape(CONV_W, 1, d), vec(conv_b), w_r.astype(BF16), vec(b_r), w_i.astype(BF16), vec(b_i), vec(la))


def _sortable_key(score):
    bits = pltpu.bitcast(score, I32)
    key = bits ^ ((bits >> 31) & 0x7FFFFFFF)
    return jnp.where(score == 0.0, 0, key)


def _kth_largest_key(count_ge, shape, topk):
    def body(it, t_cur):
        cand = t_cur ^ lax.shift_left(jnp.int32(1), 31 - it)
        return jnp.where(count_ge(cand) >= topk, cand, t_cur)
    return lax.fori_loop(0, 32, body, jnp.full(shape, INT_MIN, I32))


def _tie_select(keys, t_sel, need, tie_count, tri, key_axis):
    eq = keys == t_sel
    eqf = jnp.where(eq, 1.0, 0.0)
    if key_axis == 1:
        pref = jnp.dot(eqf.astype(BF16), tri, preferred_element_type=F32)
    else:
        pref = jnp.dot(tri, eqf.astype(BF16), preferred_element_type=F32)
    take = jnp.where(eq, jnp.where(tie_count + pref < need, 1.0, 0.0), 0.0)
    self = jnp.where(keys > t_sel, 1.0, take)
    self = jnp.where(keys == INT_MIN, 0.0, self)
    return self, tie_count + jnp.sum(eqf, axis=key_axis, keepdims=True)


def _strict_tri(n, upper):
    r = lax.broadcasted_iota(I32, (n, n), 0)
    c = lax.broadcasted_iota(I32, (n, n), 1)
    return jnp.where((r < c) if upper else (c < r), 1.0, 0.0).astype(BF16)


def _dsa_prompt_kernel(tq, tk, topk, q_ref, qi_ref, wi_ref, k_ref, vt_ref, ki_ref, o_ref,
                       key_ref, qs_ref, acc_ref, m_ref, l_ref):
    i = pl.program_id(1)
    n_kb = (i * tq) // tk + tq // tk
    reps = DSA_HEADS // DSA_KV_HEADS
    lane = lax.broadcasted_iota(I32, (tq, LANE_V7X), 1)
    low = lane < HEAD_DIM

    qi = qi_ref[...]
    wi_t = wi_ref[...].T
    qih = []
    for h in range(IDX_HEADS):
        t = qi[:, (h // 2) * LANE_V7X:(h // 2 + 1) * LANE_V7X]
        if h % 2:
            t = pltpu.roll(t, HEAD_DIM, 1)
        qih.append(t.astype(BF16))

    def a_step(kb, carry):
        start = pl.multiple_of(kb * tk, tk)
        ki = ki_ref[pl.ds(start, tk), :]
        sc = jnp.zeros((tk, tq), F32)
        for h in range(IDX_HEADS):
            dd = lax.dot_general(ki, qih[h], _NT, preferred_element_type=F32)
            sc = sc + jnp.maximum(dd, 0.0) * wi_t[h:h + 1, :]
        sc = sc * (IDX_HEADS ** -0.5)
        kpos = kb * tk + lax.broadcasted_iota(I32, (tk, tq), 0)
        qpos = i * tq + lax.broadcasted_iota(I32, (tk, tq), 1)
        key_ref[kb] = jnp.where(kpos <= qpos, _sortable_key(sc), INT_MIN)
        return carry

    lax.fori_loop(0, n_kb, a_step, 0)

    def count_where(pred):
        def body(kb, c):
            hit = jnp.where(pred(key_ref[kb]), 1.0, 0.0)
            return c + jnp.sum(hit.reshape(tk // SUBLANE_V7X, SUBLANE_V7X, tq), axis=0)
        c = lax.fori_loop(0, n_kb, body, jnp.zeros((SUBLANE_V7X, tq), F32))
        return jnp.sum(c, axis=0, keepdims=True)

    t_sel = _kth_largest_key(lambda cand: count_where(lambda keys: keys >= cand), (1, tq), topk)
    need = topk - count_where(lambda keys: keys > t_sel)

    q = q_ref[...]
    for g in range(DSA_KV_HEADS):
        for r in range(reps):
            h = reps * g + r
            x = q[:, (h // 2) * LANE_V7X:(h // 2 + 1) * LANE_V7X]
            if h % 2 != g % 2:
                x = pltpu.roll(x, HEAD_DIM, 1)
            x = jnp.where(low if g % 2 == 0 else jnp.logical_not(low), x, 0.0)
            qs_ref[g, r * tq:(r + 1) * tq, :] = x.astype(BF16)
    m_ref[...] = jnp.full(m_ref.shape, NEG, F32)
    l_ref[...] = jnp.zeros(l_ref.shape, F32)
    acc_ref[...] = jnp.zeros(acc_ref.shape, F32)
    lower = _strict_tri(tk, upper=False)

    def c_step(kb, tie_count):
        start = pl.multiple_of(kb * tk, tk)
        self, tie_count = _tie_select(key_ref[kb], t_sel, need, tie_count, lower, 0)
        sel = jnp.tile(self, (1, reps)) > 0.5
        k = k_ref[pl.ds(start, tk), :]
        vt = vt_ref[0, kb]
        for g in range(DSA_KV_HEADS):
            pair = slice((g // 2) * LANE_V7X, (g // 2 + 1) * LANE_V7X)
            s_t = lax.dot_general(k[:, pair], qs_ref[g], _NT, preferred_element_type=F32)
            s_t = jnp.where(sel, s_t, NEG)
            _flash_update_t(s_t, sel, vt[pair, :], m_ref, l_ref, acc_ref, idx=g)
        return tie_count

    lax.fori_loop(0, n_kb, c_step, jnp.zeros((1, tq), F32))

    pieces = []
    for g in range(DSA_KV_HEADS):
        og = acc_ref[g] / l_ref[g]
        for r in range(reps):
            pieces.append(og[(g % 2) * HEAD_DIM:(g % 2 + 1) * HEAD_DIM, r * tq:(r + 1) * tq])
    o_ref[...] = jnp.concatenate(pieces, axis=0).T.astype(BF16)


DSA_COL_Q = 0
DSA_COL_QI = 1024
DSA_COL_K = 1536
DSA_COL_KI = 1792
DSA_ROPE_COLS = 2048
DSA_COL_V = 2048
DSA_COL_WI = 2304
DSA_COLS = 2560


def dsa_attention_prompt(pf, pb, vt, *, batch, seq, tq, tk):
    nq = seq // tq
    nkb = seq // tk
    topk = min(DSA_TOPK_MAX, seq // 4)
    assert tq % tk == 0 and seq % tq == 0 and tq % LANE_V7X == 0
    reps = DSA_HEADS // DSA_KV_HEADS
    dkv = DSA_KV_HEADS * HEAD_DIM
    kern = functools.partial(_dsa_prompt_kernel, tq, tk, topk)
    return pl.pallas_call(
        kern,
        grid=(batch, nq),
        in_specs=[
            pl.BlockSpec((tq, 1024), lambda b, i: (b * nq + i, DSA_COL_Q // 1024)),
            pl.BlockSpec((tq, 512), lambda b, i: (b * nq + i, DSA_COL_QI // 512)),
            pl.BlockSpec((tq, LANE_V7X), lambda b, i: (b * nq + i, DSA_COL_WI // LANE_V7X)),
            pl.BlockSpec((seq, dkv), lambda b, i: (b, DSA_COL_K // dkv)),
            pl.BlockSpec((1, nkb, dkv, tk), lambda b, i: (b, 0, 0, 0)),
            pl.BlockSpec((seq, LANE_V7X), lambda b, i: (b, DSA_COL_KI // LANE_V7X)),
        ],
        out_specs=pl.BlockSpec((tq, D_MODEL), lambda b, i: (b * nq + i, 0)),
        out_shape=jax.ShapeDtypeStruct((batch * seq, D_MODEL), BF16),
        scratch_shapes=[
            pltpu.VMEM((nkb, tk, tq), I32),
            pltpu.VMEM((DSA_KV_HEADS, reps * tq, LANE_V7X), BF16),
            pltpu.VMEM((DSA_KV_HEADS, LANE_V7X, reps * tq), F32),
            pltpu.VMEM((DSA_KV_HEADS, 1, reps * tq), F32),
            pltpu.VMEM((DSA_KV_HEADS, 1, reps * tq), F32),
        ],
        compiler_params=_cparams(("parallel", "arbitrary")),
        name="dsa_attention_prompt",
    )(pf, pf, pf, pb, vt, pb)


def _dsa_select_kernel(pages_per_step, n_page_steps, topk, pt_ref, *refs):
    P = pages_per_step
    qi_ref, w_ref = refs[0], refs[1]
    kp = refs[2:2 + P]
    kinew_ref, o_ref, key_ref = refs[2 + P:]
    s_id = pl.program_id(1)
    n_chunks = n_page_steps * P + 1
    nq = SUBLANE_V7X

    def chunk_score(kit):
        dd = jnp.dot(qi_ref[0], kit, preferred_element_type=F32)
        dd = jnp.maximum(dd, 0.0) * w_ref[0][:, 0:1]
        sc = dd[0:nq]
        for h in range(1, IDX_HEADS):
            sc = sc + dd[h * nq:(h + 1) * nq]
        return sc * (IDX_HEADS ** -0.5)

    @pl.when(s_id < n_page_steps)
    def _():
        keys = _sortable_key(chunk_score(jnp.concatenate([r[0].astype(BF16) for r in kp], axis=1)))
        for p in range(P):
            key_ref[s_id * P + p] = keys[:, p * PAGE:(p + 1) * PAGE]

    @pl.when(s_id == n_page_steps)
    def _():
        row = lax.broadcasted_iota(I32, (nq, PAGE), 0)
        col = lax.broadcasted_iota(I32, (nq, PAGE), 1)
        key_ref[n_chunks - 1] = jnp.where(col <= row, _sortable_key(chunk_score(kinew_ref[0].astype(BF16))), INT_MIN)

        def count_where(pred):
            c = jnp.zeros((nq, PAGE), F32)
            for ch in range(n_chunks):
                c = c + jnp.where(pred(key_ref[ch]), 1.0, 0.0)
            return jnp.sum(c, axis=1, keepdims=True)

        t_sel = _kth_largest_key(lambda cand: count_where(lambda keys: keys >= cand), (nq, 1), topk)
        need = topk - count_where(lambda keys: keys > t_sel)
        upper = _strict_tri(PAGE, upper=True)
        tie_count = jnp.zeros((nq, 1), F32)
        for ch in range(n_chunks):
            self, tie_count = _tie_select(key_ref[ch], t_sel, need, tie_count, upper, 1)
            o_ref[0, ch] = self


def dsa_select_sample(qi_stack, w_stack, pool_kit, layer, page_table, ki_new_t, *, pages_per_step):
    n_seq, n_pages = page_table.shape
    P = pages_per_step
    nps = n_pages // P
    topk = min(DSA_TOPK_MAX, (n_pages * PAGE + SUBLANE_V7X) // 4)
    pt = (page_table + layer).reshape(-1).astype(I32)
    rows = IDX_HEADS * SUBLANE_V7X
    fixed = lambda b, s, pt_ref: (b, 0, 0)
    in_specs = [pl.BlockSpec((1, rows, HEAD_DIM), fixed), pl.BlockSpec((1, rows, PAGE), fixed)]
    in_specs += [pl.BlockSpec((1, HEAD_DIM, PAGE), m) for m in _page_maps(n_pages, P, nps, 3)]
    in_specs += [pl.BlockSpec((1, HEAD_DIM, PAGE), fixed)]
    kern = functools.partial(_dsa_select_kernel, P, nps, topk)
    return pl.pallas_call(
        kern,
        grid_spec=pltpu.PrefetchScalarGridSpec(
            num_scalar_prefetch=1,
            grid=(n_seq, nps + 1),
            in_specs=in_specs,
            out_specs=pl.BlockSpec((1, n_pages + 1, SUBLANE_V7X, PAGE), lambda b, s, pt_ref: (b, 0, 0, 0)),
            scratch_shapes=[pltpu.VMEM((n_pages + 1, SUBLANE_V7X, PAGE), I32)],
        ),
        out_shape=jax.ShapeDtypeStruct((n_seq, n_pages + 1, SUBLANE_V7X, PAGE), F32),
        compiler_params=_cparams(("parallel", "arbitrary")),
        name="dsa_select_sample",
    )(pt, qi_stack, w_stack, *([pool_kit] * P), ki_new_t)


ROUTER_COLS = LANE_V7X
TOKEN_TILE = D_MODEL // LANE_V7X


def _store_token_tiles(ref, val, *lead):
    rows = val.shape[0]
    for c in range(TOKEN_TILE):
        ref[lead + (pl.ds(c, rows, stride=TOKEN_TILE), slice(None))] = val[:, c * LANE_V7X:(c + 1) * LANE_V7X]


def _load_token_tiles(ref, rows, *lead):
    return jnp.concatenate([ref[lead + (pl.ds(c, rows, stride=TOKEN_TILE), slice(None))]
                            for c in range(TOKEN_TILE)], axis=1)


def _router_kernel(x_ref, g_ref, w_ref, b_ref, xn_ref, info_ref, cnt_ref, carry_ref):
    i = pl.program_id(0)

    @pl.when(i == 0)
    def _():
        carry_ref[...] = jnp.zeros(carry_ref.shape, F32)

    x = x_ref[...]
    xn = x * lax.rsqrt(jnp.mean(x * x, axis=-1, keepdims=True) + NORM_EPS) * g_ref[...]
    _store_token_tiles(xn_ref, xn)
    lg = jnp.dot(xn, w_ref[...], preferred_element_type=F32, precision=lax.Precision.HIGHEST) + b_ref[...]
    tm = lg.shape[0]
    lane = lax.broadcasted_iota(I32, lg.shape, 1)
    lanef = lane.astype(F32)
    big = float(ROUTER_COLS)

    grp_lanes = lane < 4
    zg = jnp.where(grp_lanes, lg, NEG)
    mg = jnp.max(zg, axis=1, keepdims=True)
    sg = jnp.sum(jnp.where(grp_lanes, jnp.exp(zg - mg), 0.0), axis=1, keepdims=True)
    g_gate = 1.0 / sg
    grp = jnp.min(jnp.where(grp_lanes & (zg == mg), lanef, big), axis=1, keepdims=True)

    in_grp = (lane >= 4) & (lane < 4 + N_EXPERTS) & (((lane - 4) >> 2).astype(F32) == grp)
    ze = jnp.where(in_grp, lg, NEG)
    l1 = jnp.max(ze, axis=1, keepdims=True)
    i1 = jnp.min(jnp.where(in_grp & (ze == l1), lanef, big), axis=1, keepdims=True)
    rest = in_grp & (lanef != i1)
    ze2 = jnp.where(rest, lg, NEG)
    l2 = jnp.max(ze2, axis=1, keepdims=True)
    i2 = jnp.min(jnp.where(rest & (ze2 == l2), lanef, big), axis=1, keepdims=True)
    e21 = jnp.exp(l2 - l1)
    gate0 = g_gate / (1.0 + e21)
    gate1 = g_gate * e21 / (1.0 + e21)
    e0 = i1 - 4.0
    e1 = i2 - 4.0

    oh0 = lanef == e0
    oh1 = lanef == e1
    ohs = jnp.where(oh0 | oh1, 1.0, 0.0)
    lower = jnp.where(lax.broadcasted_iota(I32, (tm, tm), 1) < lax.broadcasted_iota(I32, (tm, tm), 0), 1.0, 0.0)
    cnt = jnp.dot(lower.astype(BF16), ohs.astype(BF16), preferred_element_type=F32) + carry_ref[0:1, :]
    rank0 = jnp.sum(jnp.where(oh0, cnt, 0.0), axis=1, keepdims=True)
    rank1 = jnp.sum(jnp.where(oh1, cnt, 0.0), axis=1, keepdims=True)
    carry_ref[0:1, :] = carry_ref[0:1, :] + jnp.sum(ohs, axis=0, keepdims=True)

    info = jnp.zeros(lg.shape, F32)
    for idx, val in enumerate((e0, e1, gate0, gate1, rank0, rank1)):
        info = jnp.where(lane == idx, val, info)
    info_ref[...] = info

    @pl.when(i == pl.num_programs(0) - 1)
    def _():
        cnt_ref[...] = carry_ref[...]


def moe_router(x, g, w_router, b_router, *, tm):
    n, d = x.shape
    return pl.pallas_call(
        _router_kernel,
        grid=(n // tm,),
        in_specs=[
            pl.BlockSpec((tm, d), lambda i: (i, 0)),
            pl.BlockSpec((1, d), lambda i: (0, 0)),
            pl.BlockSpec((d, ROUTER_COLS), lambda i: (0, 0)),
            pl.BlockSpec((1, ROUTER_COLS), lambda i: (0, 0)),
        ],
        out_specs=[
            pl.BlockSpec((tm * TOKEN_TILE, LANE_V7X), lambda i: (i, 0)),
            pl.BlockSpec((tm, ROUTER_COLS), lambda i: (i, 0)),
            pl.BlockSpec((SUBLANE_V7X, ROUTER_COLS), lambda i: (0, 0)),
        ],
        out_shape=[
            jax.ShapeDtypeStruct((n * TOKEN_TILE, LANE_V7X), F32),
            jax.ShapeDtypeStruct((n, ROUTER_COLS), F32),
            jax.ShapeDtypeStruct((SUBLANE_V7X, ROUTER_COLS), F32),
        ],
        scratch_shapes=[pltpu.VMEM((SUBLANE_V7X, ROUTER_COLS), F32)],
        compiler_params=_cparams(("arbitrary",)),
        name="moe_router",
    )(x, g.reshape(1, d), w_router, b_router)


def _expert_kernel(spare_row0, be_ref, nu_ref, gcur_ref, gnext_ref, sidx_ref, x_hbm, wup_ref, wdn_ref, y_hbm,
                   xbuf, ybuf, gsem, ssem):
    del be_ref
    i = pl.program_id(0)
    n_used = nu_ref[0]
    slot = i % 2
    tt = TOKEN_TILE
    mb = xbuf.shape[1] // tt

    def tile_rows(start):
        return pl.ds(pl.multiple_of(start, tt), tt)

    def start_gather(idx_ref, dst_slot):
        def body(r, carry):
            pltpu.make_async_copy(x_hbm.at[tile_rows(idx_ref[0, 0, r]), :], xbuf.at[dst_slot, tile_rows(r * tt), :],
                                  gsem.at[dst_slot]).start()
            return carry
        lax.fori_loop(0, mb, body, 0, unroll=8)

    def scatter_block(src_slot):
        return pltpu.make_async_copy(ybuf.at[src_slot], y_hbm.at[pl.ds(0, mb * tt), :], ssem.at[src_slot])

    @pl.when(i == 0)
    def _():
        ybuf[0] = jnp.zeros(ybuf.shape[1:], F32)
        for s in range(2):
            spare = pltpu.make_async_copy(ybuf.at[0], y_hbm.at[pl.ds(spare_row0 + s * mb * tt, mb * tt), :],
                                          ssem.at[0])
            spare.start()
            spare.wait()

    @pl.when((i == 0) & (n_used > 0))
    def _():
        start_gather(gcur_ref, 0)

    @pl.when(i + 1 < n_used)
    def _():
        start_gather(gnext_ref, 1 - slot)

    @pl.when(i < n_used)
    def _():
        pltpu.make_async_copy(x_hbm.at[pl.ds(0, mb * tt), :], xbuf.at[slot], gsem.at[slot]).wait()
        x = _load_token_tiles(xbuf, mb, slot).astype(BF16)
        h = jnp.dot(x, wup_ref[0], preferred_element_type=F32)
        a = (jax.nn.silu(h[:, :EXPERT_FF]) * h[:, EXPERT_FF:]).astype(BF16)
        _store_token_tiles(ybuf, jnp.dot(a, wdn_ref[0], preferred_element_type=F32), slot)

        def issue(r, carry):
            pltpu.make_async_copy(ybuf.at[slot, tile_rows(r * tt), :], y_hbm.at[tile_rows(sidx_ref[0, 0, r]), :],
                                  ssem.at[slot]).start()
            return carry
        lax.fori_loop(0, mb, issue, 0, unroll=8)

        @pl.when(i >= 1)
        def _():
            scatter_block(1 - slot).wait()

        @pl.when(i == n_used - 1)
        def _():
            scatter_block(slot).wait()


def moe_experts(xn, w_up, w_down, block_e, n_used, gidx, sidx, *, n_rows_out):
    n_blocks = block_e.shape[0]
    mb = MOE_ROWS
    d = D_MODEL
    tt = TOKEN_TILE
    smem_blk = lambda f: pl.BlockSpec((1, 1, mb), f, memory_space=pltpu.SMEM)
    return pl.pallas_call(
        functools.partial(_expert_kernel, (n_rows_out - 2 * mb) * tt),
        grid_spec=pltpu.PrefetchScalarGridSpec(
            num_scalar_prefetch=2,
            grid=(n_blocks,),
            in_specs=[
                smem_blk(lambda i, be, nu: (i, 0, 0)),
                smem_blk(lambda i, be, nu: (jnp.minimum(i + 1, n_blocks - 1), 0, 0)),
                smem_blk(lambda i, be, nu: (i, 0, 0)),
                pl.BlockSpec(memory_space=pl.ANY),
                pl.BlockSpec((1, d, 2 * EXPERT_FF), lambda i, be, nu: (be[i], 0, 0)),
                pl.BlockSpec((1, EXPERT_FF, d), lambda i, be, nu: (be[i], 0, 0)),
            ],
            out_specs=pl.BlockSpec(memory_space=pl.ANY),
            scratch_shapes=[
                pltpu.VMEM((2, mb * tt, LANE_V7X), F32),
                pltpu.VMEM((2, mb * tt, LANE_V7X), F32),
                pltpu.SemaphoreType.DMA((2,)),
                pltpu.SemaphoreType.DMA((2,)),
            ],
        ),
        out_shape=jax.ShapeDtypeStruct((n_rows_out * tt, LANE_V7X), F32),
        compiler_params=_cparams(("arbitrary",)),
        name="moe_experts",
    )(block_e, n_used, gidx.reshape(n_blocks, 1, mb), gidx.reshape(n_blocks, 1, mb),
      sidx.reshape(n_blocks, 1, mb), xn, w_up, w_down)


def _combine_kernel(final_norm, x_ref, y0_ref, y1_ref, info_ref, g_ref, o_ref):
    info = info_ref[...]
    rows = x_ref.shape[0]
    x = (x_ref[...] + info[:, 2:3] * _load_token_tiles(y0_ref, rows)
         + info[:, 3:4] * _load_token_tiles(y1_ref, rows))
    if final_norm:
        x = x * lax.rsqrt(jnp.mean(x * x, axis=-1, keepdims=True) + NORM_EPS) * g_ref[...]
    o_ref[...] = x


def moe_combine(x, y, info, g_final, *, final_norm, tm):
    n, d = x.shape
    nt = n // tm
    return pl.pallas_call(
        functools.partial(_combine_kernel, final_norm),
        grid=(nt,),
        in_specs=[
            pl.BlockSpec((tm, d), lambda i: (i, 0)),
            pl.BlockSpec((tm * TOKEN_TILE, LANE_V7X), lambda i: (i, 0)),
            pl.BlockSpec((tm * TOKEN_TILE, LANE_V7X), lambda i: (nt + i, 0)),
            pl.BlockSpec((tm, ROUTER_COLS), lambda i: (i, 0)),
            pl.BlockSpec((1, d), lambda i: (0, 0)),
        ],
        out_specs=pl.BlockSpec((tm, d), lambda i: (i, 0)),
        out_shape=jax.ShapeDtypeStruct((n, d), F32),
        compiler_params=_cparams(("parallel",)),
        name="moe_combine",
    )(x, y, y, info, g_final.reshape(1, d))


def moe_layer(x, g_ffn, w_group, b_group, w_expert, b_expert, w_up, w_down, g_final, *, final_norm):
    n, d = x.shape
    mb = MOE_ROWS
    w_router = jnp.zeros((d, ROUTER_COLS), F32).at[:, :4].set(w_group).at[:, 4:4 + N_EXPERTS].set(w_expert)
    b_router = jnp.zeros((1, ROUTER_COLS), F32).at[0, :4].set(b_group).at[0, 4:4 + N_EXPERTS].set(b_expert)
    xn, info, cnt = moe_router(x, g_ffn, w_router, b_router, tm=256)

    e01 = info[:, 0:2].astype(I32)
    rank01 = info[:, 4:6].astype(I32)
    counts = cnt[0, :N_EXPERTS].astype(I32)
    padded = (counts + mb - 1) // mb * mb
    pend = jnp.cumsum(padded)
    pstart = pend - padded
    dest = pstart[e01] + rank01
    n_blocks = (2 * n + N_EXPERTS * (mb - 1) + mb - 1) // mb
    blk = jnp.arange(n_blocks, dtype=I32)[:, None]
    spare = 2 * n + (blk % 2) * mb + jnp.arange(mb, dtype=I32)[None, :]
    target = (jnp.arange(n, dtype=I32)[:, None] + jnp.array([0, n], I32)[None, :]).reshape(-1)
    sidx = spare.reshape(-1).at[dest.reshape(-1)].set(target)
    gidx = jnp.where(sidx < 2 * n, sidx % n, 0)
    block_e = jnp.minimum(jnp.searchsorted(pend, blk[:, 0] * mb, side="right"), N_EXPERTS - 1).astype(I32)
    n_used = (pend[-1] // mb).astype(I32).reshape(1)

    y = moe_experts(xn, w_up, w_down, block_e, n_used, gidx * TOKEN_TILE, sidx * TOKEN_TILE,
                    n_rows_out=2 * n + 2 * mb)
    return moe_combine(x, y, info, g_final, final_norm=final_norm, tm=256)


def _rope_tables(pos):
    inv = ROPE_THETA ** (-jnp.arange(0, HEAD_DIM, 2, dtype=F32) / HEAD_DIM)
    ang = pos.astype(F32)[:, None] * inv[None, :]
    cos = jnp.tile(jnp.cos(ang), (1, 2 * LANE_V7X // HEAD_DIM))
    sin = jnp.sin(ang)
    sin = jnp.tile(jnp.concatenate([-sin, sin], axis=1), (1, LANE_V7X // HEAD_DIM))
    return cos, sin


def _largest_divisor(n, candidates):
    for c in candidates:
        if n % c == 0:
            return c
    raise ValueError(f"no tile in {candidates} divides {n}")


def _pad_last(a, size):
    return jnp.pad(a, [(0, 0)] * (a.ndim - 1) + [(0, size - a.shape[-1])])


def kernel(x_prompt, x_sample, cache_diff_k, cache_diff_v, state_lru_h, state_lru_conv, cache_dsa_k, cache_dsa_v,
           cache_dsa_kidx, page_table, norm_mix, norm_ffn, norm_final, diff_w_in, diff_lambda, diff_subln,
           diff_w_out, lru_w_in, lru_conv_w, lru_conv_b, lru_w_r, lru_b_r, lru_w_i, lru_b_i, lru_a, lru_w_out,
           dsa_w_in, dsa_w_out, moe_w_group, moe_b_group, moe_w_expert, moe_b_expert, moe_w_up, moe_w_down):
    batch, seq, d = x_prompt.shape
    nseq, dseq, _ = x_sample.shape
    assert dseq == SUBLANE_V7X and d == D_MODEL
    np_, ns_ = batch * seq, nseq * dseq
    n = np_ + ns_
    depth = norm_mix.shape[0]
    n_pool = cache_diff_k.shape[1]
    tm = _largest_divisor(n, (1280, 640, 256))
    tm_res = _largest_divisor(n, (640, 256))
    n_pages = page_table.shape[1]
    past = n_pages * PAGE
    t_diff = _largest_divisor(seq, (512, 256))
    t_dsa = 256
    dkv = DSA_KV_HEADS * HEAD_DIM

    x = jnp.concatenate([x_prompt.reshape(np_, d), x_sample.reshape(ns_, d)], axis=0)
    pos = jnp.concatenate([jnp.tile(jnp.arange(seq, dtype=I32), batch),
                           jnp.tile(past + jnp.arange(dseq, dtype=I32), nseq)])
    cos, sin = _rope_tables(pos)
    scale = HEAD_DIM ** -0.5

    pool_diff_kt = cache_diff_k.transpose(0, 1, 3, 4, 2).reshape(-1, 2 * DIFF_HEADS, HEAD_DIM, PAGE)
    pool_diff_v = cache_diff_v.reshape(-1, PAGE * DIFF_HEADS, 2 * HEAD_DIM)
    pool_dsa_kt = cache_dsa_k.transpose(0, 1, 3, 4, 2).reshape(-1, DSA_KV_HEADS, HEAD_DIM, PAGE)
    pool_dsa_vt = cache_dsa_v.transpose(0, 1, 3, 4, 2).reshape(-1, DSA_KV_HEADS, HEAD_DIM, PAGE)
    pool_dsa_kit = cache_dsa_kidx.transpose(0, 1, 3, 2).reshape(-1, HEAD_DIM, PAGE)

    outs = {k: [] for k in ("dk_p", "dv_p", "dk_s", "dv_s", "lh_p", "lc_p", "lh_s", "lc_s",
                            "sk_p", "sv_p", "si_p", "sk_s", "sv_s", "si_s")}

    for i in range(depth):
        kind, j = i % 3, i // 3
        if kind == 0:
            lam_init = 0.8 - 0.6 * math.exp(-0.3 * i)
            lp = diff_lambda[j]
            lam = jnp.exp(jnp.sum(lp[0] * lp[1])) - jnp.exp(jnp.sum(lp[2] * lp[3])) + lam_init
            w = jnp.concatenate([diff_w_in[j][:, :d] * scale, diff_w_in[j][:, d:]], axis=1).astype(BF16)
            pf, pb = norm_matmul(x, norm_mix[i], w, cos, sin, n_rope_cols=2 * d, tm=tm, tn=512)
            k_all, v_all = pf[:, d:2 * d], pf[:, 2 * d:]
            outs["dk_p"].append(k_all[:np_].reshape(batch, seq, 2 * DIFF_HEADS, HEAD_DIM))
            outs["dv_p"].append(v_all[:np_].reshape(batch, seq, DIFF_HEADS, 2 * HEAD_DIM))
            outs["dk_s"].append(k_all[np_:].reshape(nseq, dseq, 2 * DIFF_HEADS, HEAD_DIM))
            outs["dv_s"].append(v_all[np_:].reshape(nseq, dseq, DIFF_HEADS, 2 * HEAD_DIM))
            vt = pb[:np_, 2 * d:].reshape(batch, seq // t_diff, t_diff, DIFF_HEADS, 2 * HEAD_DIM)
            vt = vt.transpose(0, 3, 1, 4, 2)
            ap = diff_attention_prompt(pb, vt, lam, diff_subln[j], 1.0 - lam_init, batch=batch, seq=seq,
                                       tq=t_diff, tk=t_diff)
            qs = pb[np_:, :d].reshape(nseq, dseq, DIFF_HEADS, 2, HEAD_DIM).transpose(0, 2, 3, 1, 4)
            eye2 = jnp.eye(2, dtype=BF16)
            q8 = (qs[:, :, :, :, None, :] * eye2[None, None, :, None, :, None]).reshape(
                nseq, DIFF_HEADS, 2 * dseq, 2 * HEAD_DIM)
            knew_t = _pad_last(k_all[np_:].reshape(nseq, dseq, 2 * DIFF_HEADS, HEAD_DIM).transpose(0, 2, 3, 1), PAGE)
            vnew = jnp.pad(v_all[np_:].reshape(nseq, dseq, DIFF_HEADS, 2 * HEAD_DIM),
                           ((0, 0), (0, PAGE - dseq), (0, 0), (0, 0))).reshape(nseq, PAGE * DIFF_HEADS, 2 * HEAD_DIM)
            a_s = diff_attention_sample(q8, pool_diff_kt, pool_diff_v, j * n_pool, page_table, knew_t, vnew,
                                        lam, diff_subln[j], 1.0 - lam_init,
                                        pages_per_step=_largest_divisor(n_pages, (8, 4, 2, 1)))
            att = jnp.concatenate([ap, a_s.reshape(ns_, d).astype(BF16)], axis=0)
            x = matmul_residual(att, diff_w_out[j].astype(BF16), x, tm=tm_res)
        elif kind == 1:
            pf, _ = norm_matmul(x, norm_mix[i], lru_w_in[j].astype(BF16), cos, sin, n_rope_cols=0, tm=tm, tn=512)
            la = 8.0 * jax.nn.log_sigmoid(lru_a[j])
            wts = (lru_conv_w[j], lru_conv_b[j], lru_w_r[j], lru_b_r[j], lru_w_i[j], lru_b_i[j], la)
            yp, hp, cp = lru_core(pf[:np_].reshape(batch, seq, 2 * d), jnp.zeros((batch, 1, d), F32),
                                  jnp.zeros((batch, SUBLANE_V7X, d), F32), *wts, gb=1, tm=256)
            c0 = jnp.pad(state_lru_conv[j], ((0, 0), (SUBLANE_V7X - (CONV_W - 1), 0), (0, 0)))
            ys, hs, cs = lru_core(pf[np_:].reshape(nseq, dseq, 2 * d), state_lru_h[j][:, None, :], c0, *wts,
                                  gb=nseq, tm=dseq)
            outs["lh_p"].append(hp[:, 0])
            outs["lc_p"].append(cp[:, SUBLANE_V7X - (CONV_W - 1):])
            outs["lh_s"].append(hs[:, 0])
            outs["lc_s"].append(cs[:, SUBLANE_V7X - (CONV_W - 1):])
            yh = jnp.concatenate([yp.reshape(np_, d), ys.reshape(ns_, d)], axis=0)
            x = matmul_residual(yh, lru_w_out[j].astype(BF16), x, tm=tm_res)
        else:
            wj = dsa_w_in[j]
            o_q, o_k, o_v, o_qi, o_ki, o_wi = 0, 1024, 1280, 1536, 2048, 2112
            w = jnp.zeros((d, DSA_COLS), F32)
            w = w.at[:, DSA_COL_Q:DSA_COL_Q + 1024].set(wj[:, o_q:o_k] * scale)
            w = w.at[:, DSA_COL_QI:DSA_COL_QI + 512].set(wj[:, o_qi:o_ki] * scale)
            w = w.at[:, DSA_COL_K:DSA_COL_K + dkv].set(wj[:, o_k:o_v])
            w = w.at[:, DSA_COL_KI:DSA_COL_KI + HEAD_DIM].set(wj[:, o_ki:o_wi])
            w = w.at[:, DSA_COL_V:DSA_COL_V + dkv].set(wj[:, o_v:o_qi])
            w = w.at[:, DSA_COL_WI:DSA_COL_WI + IDX_HEADS].set(wj[:, o_wi:])
            pf, pb = norm_matmul(x, norm_mix[i], w.astype(BF16), cos, sin, n_rope_cols=DSA_ROPE_COLS, tm=tm, tn=512)
            k_all = pf[:, DSA_COL_K:DSA_COL_K + dkv]
            v_all = pf[:, DSA_COL_V:DSA_COL_V + dkv]
            ki_all = pf[:, DSA_COL_KI:DSA_COL_KI + HEAD_DIM]
            outs["sk_p"].append(k_all[:np_].reshape(batch, seq, DSA_KV_HEADS, HEAD_DIM))
            outs["sv_p"].append(v_all[:np_].reshape(batch, seq, DSA_KV_HEADS, HEAD_DIM))
            outs["si_p"].append(ki_all[:np_].reshape(batch, seq, HEAD_DIM))
            outs["sk_s"].append(k_all[np_:].reshape(nseq, dseq, DSA_KV_HEADS, HEAD_DIM))
            outs["sv_s"].append(v_all[np_:].reshape(nseq, dseq, DSA_KV_HEADS, HEAD_DIM))
            outs["si_s"].append(ki_all[np_:].reshape(nseq, dseq, HEAD_DIM))
            vt = pb[:np_, DSA_COL_V:DSA_COL_V + dkv].reshape(batch, seq // t_dsa, t_dsa, dkv).transpose(0, 1, 3, 2)
            ap = dsa_attention_prompt(pf, pb, vt, batch=batch, seq=seq, tq=t_dsa, tk=t_dsa)
            ps = pf[np_:]
            qi_stack = ps[:, DSA_COL_QI:DSA_COL_QI + 512].reshape(nseq, dseq, IDX_HEADS, HEAD_DIM)
            qi_stack = qi_stack.transpose(0, 2, 1, 3).reshape(nseq, IDX_HEADS * dseq, HEAD_DIM).astype(BF16)
            w_stack = ps[:, DSA_COL_WI:DSA_COL_WI + IDX_HEADS].reshape(nseq, dseq, IDX_HEADS).transpose(0, 2, 1)
            w_stack = jnp.broadcast_to(w_stack.reshape(nseq, IDX_HEADS * dseq, 1), (nseq, IDX_HEADS * dseq, PAGE))
            ki_new_t = _pad_last(ki_all[np_:].reshape(nseq, dseq, HEAD_DIM).transpose(0, 2, 1), PAGE)
            mask = dsa_select_sample(qi_stack, w_stack, pool_dsa_kit, j * n_pool, page_table, ki_new_t,
                                     pages_per_step=_largest_divisor(n_pages, (16, 8, 4, 2, 1)))
            q4 = ps[:, :d].reshape(nseq, dseq, DSA_HEADS, HEAD_DIM).transpose(0, 2, 1, 3)
            q4 = q4.reshape(nseq, DSA_KV_HEADS, (DSA_HEADS // DSA_KV_HEADS) * dseq, HEAD_DIM).astype(BF16)
            new_t = lambda a: _pad_last(a[np_:].reshape(nseq, dseq, DSA_KV_HEADS, HEAD_DIM).transpose(0, 2, 3, 1), PAGE)
            o4 = dsa_attention_sample(q4, pool_dsa_kt, pool_dsa_vt, j * n_pool, page_table, new_t(k_all),
                                      new_t(v_all), mask, pages_per_step=_largest_divisor(n_pages, (8, 4, 2, 1)))
            a_s = o4.reshape(nseq, DSA_HEADS, dseq, HEAD_DIM).transpose(0, 2, 1, 3).reshape(ns_, d)
            att = jnp.concatenate([ap, a_s.astype(BF16)], axis=0)
            x = matmul_residual(att, dsa_w_out[j].astype(BF16), x, tm=tm_res)

        x = moe_layer(x, norm_ffn[i], moe_w_group[i], moe_b_group[i], moe_w_expert[i], moe_b_expert[i],
                      moe_w_up[i].astype(BF16), moe_w_down[i].astype(BF16), norm_final,
                      final_norm=(i == depth - 1))

    st = lambda key: jnp.stack(outs[key])
    return (x[:np_].reshape(batch, seq, d), x[np_:].reshape(nseq, dseq, d),
            st("dk_p"), st("dv_p"), st("dk_s"), st("dv_s"),
            st("lh_p"), st("lc_p"), st("lh_s"), st("lc_s"),
            st("sk_p"), st("sv_p"), st("si_p"), st("sk_s"), st("sv_s"), st("si_s"))
```
